```python
import math
import jax, jax.numpy as jnp
from jax import lax
import numpy as np

D_MODEL = 2048
BATCH = 4
SEQ = 2048
DEPTH = 4
DEC_BATCH = 8
DEC_SEQ = 4096
PAST_LEN = 128

GRID_W = 64
N_MIXERS = 3
N_HEADS = 16
HEAD_DIM = D_MODEL // N_HEADS
WIN_ROWS_MAX = 8
WIN_COLS = 16
COL_BLOCK = 16
KEY_COLS = 2 * COL_BLOCK
N_COL_BLOCKS = GRID_W // COL_BLOCK
POOL_WINDOWS = (2, 4, 8, 16)
N_POOL_GROUPS = len(POOL_WINDOWS)
POOL_GROUP_DIM = D_MODEL // N_POOL_GROUPS
CONV_WIDTH = 3
N_EXPERTS = 16
N_EXPERT_GROUPS = 4
EXPERTS_PER_GROUP = N_EXPERTS // N_EXPERT_GROUPS
TOP_K = 2
D_EXPERT = D_MODEL // 2
MOE_BLOCK = 128
ALPHA = (2 * DEPTH) ** 0.25
BETA = (8 * DEPTH) ** -0.25
LN_EPS = 1e-5
N_ATTN_LAYERS = (DEPTH + N_MIXERS - 1) // N_MIXERS
N_POOL_LAYERS = (DEPTH - 1 + N_MIXERS - 1) // N_MIXERS
N_CONV_LAYERS = (DEPTH - 2 + N_MIXERS - 1) // N_MIXERS

kernel_name = "hybrid_natten_pool_shortconv_grouped_moe_encoder"


def layer_norm(x, g, b):
    xf = x.astype(jnp.float32)
    mu = jnp.mean(xf, axis=-1, keepdims=True)
    var = jnp.mean(jnp.square(xf - mu), axis=-1, keepdims=True)
    return ((xf - mu) * lax.rsqrt(var + LN_EPS) * g + b).astype(x.dtype)


def neighbourhood_attention(x, w_qkv, w_o, rpb):
    B, L, D = x.shape
    rows = L // GRID_W
    win_rows = min(WIN_ROWS_MAX, rows)
    qkv = jnp.einsum('bld,de->ble', x, w_qkv).reshape(B, rows, GRID_W, 3, N_HEADS, HEAD_DIM)
    q = qkv[:, :, :, 0] * (HEAD_DIM ** -0.5)
    k = qkv[:, :, :, 1]
    v = qkv[:, :, :, 2]
    qc = np.arange(GRID_W).reshape(N_COL_BLOCKS, COL_BLOCK)
    kc0 = np.clip(np.arange(N_COL_BLOCKS) * COL_BLOCK - WIN_COLS // 2, 0, GRID_W - KEY_COLS)
    kc = kc0[:, None] + np.arange(KEY_COLS)[None, :]
    c_start = np.clip(qc - WIN_COLS // 2, 0, GRID_W - WIN_COLS)
    col_mask = (kc[:, None, :] >= c_start[:, :, None]) & (kc[:, None, :] < c_start[:, :, None] + WIN_COLS)
    dx_idx = np.clip(kc[:, None, :] - qc[:, :, None] + WIN_COLS - 1, 0, 2 * WIN_COLS - 2)
    rpb_cols = rpb[:, :, dx_idx]
    mask = jnp.asarray(col_mask)[None, None, :, :, None, :]

    def row_step(r):
        start = jnp.clip(r - win_rows // 2, 0, rows - win_rows)
        q_r = lax.dynamic_index_in_dim(q, r, axis=1, keepdims=False)
        q_r = q_r.reshape(B, N_COL_BLOCKS, COL_BLOCK, N_HEADS, HEAD_DIM)
        k_r = lax.dynamic_slice_in_dim(k, start, win_rows, axis=1)[:, :, kc]
        v_r = lax.dynamic_slice_in_dim(v, start, win_rows, axis=1)[:, :, kc]
        s = jnp.einsum('bjqhd,bajkhd->bhjqak', q_r, k_r, preferred_element_type=jnp.float32)
        dy_idx = start + jnp.arange(win_rows) - r + WIN_ROWS_MAX - 1
        bias = jnp.transpose(rpb_cols[:, dy_idx], (0, 2, 3, 1, 4)).astype(jnp.float32)
        s = jnp.where(mask, s + bias[None], jnp.float32(-1e30))
        p = jax.nn.softmax(s.reshape(B, N_HEADS, N_COL_BLOCKS, COL_BLOCK, win_rows * KEY_COLS), axis=-1)
        p = p.reshape(s.shape).astype(v.dtype)
        o = jnp.einsum('bhjqak,bajkhd->bjqhd', p, v_r)
        return o.reshape(B, GRID_W, D)

    o = lax.map(row_step, jnp.arange(rows))
    o = jnp.transpose(o, (1, 0, 2, 3)).reshape(B, L, D)
    return jnp.einsum('bld,de->ble', o, w_o)


def centred_pool_minus_self(xg, window):
    B, L, C = xg.shape
    xf = xg.astype(jnp.float32)
    cs = jnp.concatenate([jnp.zeros((B, 1, C), jnp.float32), lax.cumsum(xf, axis=1)], axis=1)
    t = jnp.arange(L)
    lo = jnp.clip(t - window // 2, 0, L)
    hi = jnp.clip(t - window // 2 + window, 0, L)
    cnt = (hi - lo).astype(jnp.float32)
    return ((cs[:, hi] - cs[:, lo]) / cnt[None, :, None] - xf).astype(xg.dtype)


def pool_mixer(x, w_pool, scale):
    B, L, D = x.shape
    xg = x.reshape(B, L, N_POOL_GROUPS, POOL_GROUP_DIM)
    pooled = jnp.stack([centred_pool_minus_self(xg[:, :, g], w) for g, w in enumerate(POOL_WINDOWS)], axis=2)
    y = jnp.einsum('blgc,gce->blge', pooled, w_pool).reshape(B, L, D)
    return y * scale


def short_conv_mixer(x, w_in, conv_w, w_out):
    L = x.shape[1]
    bch = jnp.einsum('bld,de->ble', x, w_in)
    b_gate, c_gate, h = jnp.split(bch, 3, axis=-1)
    u = jnp.pad(c_gate * h, ((0, 0), (1, 1), (0, 0)))
    conv = sum(u[:, j:j + L] * conv_w[j] for j in range(CONV_WIDTH))
    return jnp.einsum('bld,de->ble', b_gate * conv, w_out)


def route(x_flat, router_w, router_b):
    n = x_flat.shape[0]
    scores = jax.nn.softmax(jnp.einsum('nd,de->ne', x_flat, router_w).astype(jnp.float32), axis=-1)
    biased = (scores + router_b.astype(jnp.float32)).reshape(n, N_EXPERT_GROUPS, EXPERTS_PER_GROUP)
    group_score = jnp.sum(lax.top_k(biased, TOP_K)[0], axis=-1)
    g_sel = jnp.argmax(group_score, axis=-1).astype(jnp.int32)
    in_group = jnp.take_along_axis(biased, g_sel[:, None, None], axis=1)[:, 0]
    _, local = lax.top_k(in_group, TOP_K)
    experts = g_sel[:, None] * EXPERTS_PER_GROUP + local.astype(jnp.int32)
    w = jnp.take_along_axis(scores, experts, axis=-1)
    return experts, w / jnp.sum(w, axis=-1, keepdims=True)


def moe(x, router_w, router_b, w_gate, w_up, w_down):
    B, L, D = x.shape
    n = B * L
    xf = x.reshape(n, D)
    experts, gates = route(xf, router_w, router_b)
    n_assign = n * TOP_K
    e_flat = experts.reshape(-1)
    tok_flat = jnp.repeat(jnp.arange(n, dtype=jnp.int32), TOP_K)
    g_flat = gates.reshape(-1)
    order = jnp.argsort(e_flat, stable=True)
    e_s, tok_s, g_s = e_flat[order], tok_flat[order], g_flat[order]
    counts = jnp.zeros((N_EXPERTS,), jnp.int32).at[e_flat].add(1)
    start = jnp.cumsum(counts) - counts
    padded = (counts + MOE_BLOCK - 1) // MOE_BLOCK * MOE_BLOCK
    pend = jnp.cumsum(padded)
    pstart = pend - padded
    dest = pstart[e_s] + jnp.arange(n_assign, dtype=jnp.int32) - start[e_s]
    n_blocks = -(-n_assign // MOE_BLOCK) + N_EXPERTS
    buf = jnp.zeros((n_blocks * MOE_BLOCK, D), x.dtype).at[dest].set(xf[tok_s])
    block_start = jnp.arange(n_blocks, dtype=jnp.int32) * MOE_BLOCK
    block_expert = jnp.minimum(jnp.searchsorted(pend, block_start, side='right'), N_EXPERTS - 1)

    def expert_block(args):
        xb, e = args
        hdn = jax.nn.silu(xb @ w_gate[e]) * (xb @ w_up[e])
        return hdn @ w_down[e]

    ys = lax.map(expert_block, (buf.reshape(n_blocks, MOE_BLOCK, D), block_expert)).reshape(-1, D)
    out = jnp.zeros((n, D), jnp.float32).at[tok_s].add(ys[dest].astype(jnp.float32) * g_s[:, None])
    return out.astype(x.dtype).reshape(B, L, D)


def trunk(x, attn_w_qkv, attn_w_o, attn_rpb, pool_w, pool_scale, conv_w_in, conv_w, conv_w_out,
          router_w, router_b, expert_w_gate, expert_w_up, expert_w_down, ln_g, ln_b):
    for i in range(DEPTH):
        kind, slot = i % N_MIXERS, i // N_MIXERS
        if kind == 0:
            m = neighbourhood_attention(x, attn_w_qkv[slot], attn_w_o[slot], attn_rpb[slot])
        elif kind == 1:
            m = pool_mixer(x, pool_w[slot], pool_scale[slot])
        else:
            m = short_conv_mixer(x, conv_w_in[slot], conv_w[slot], conv_w_out[slot])
        x = layer_norm(ALPHA * x + m, ln_g[i, 0], ln_b[i, 0])
        f = moe(x, router_w, router_b, expert_w_gate[i], expert_w_up[i], expert_w_down[i])
        x = layer_norm(ALPHA * x + f, ln_g[i, 1], ln_b[i, 1])
    return x


def setup_inputs(seed: int = 0) -> dict:
    key = jax.random.key(seed)
    ks = jax.random.split(key, 18)
    nrm = jax.random.normal
    f32 = jnp.float32
    D = D_MODEL
    x_prompt = nrm(ks[0], (BATCH, SEQ, D), f32)
    x_sample = nrm(ks[1], (DEC_BATCH, DEC_SEQ, D), f32)
    qkv_scale = jnp.concatenate([jnp.ones((2 * D,), f32), jnp.full((D,), BETA, f32)])
    attn_w_qkv = nrm(ks[2], (N_ATTN_LAYERS, D, 3 * D), f32) * (D ** -0.5) * qkv_scale
    attn_w_o = nrm(ks[3], (N_ATTN_LAYERS, D, D), f32) * (D ** -0.5) * BETA
    attn_rpb = nrm(ks[4], (N_ATTN_LAYERS, N_HEADS, 2 * WIN_ROWS_MAX - 1, 2 * WIN_COLS - 1), f32) * 0.1
    pool_w = nrm(ks[5], (N_POOL_LAYERS, N_POOL_GROUPS, POOL_GROUP_DIM, POOL_GROUP_DIM), f32) * (POOL_GROUP_DIM ** -0.5) * BETA
    pool_scale = 1.0 + 0.1 * nrm(ks[6], (N_POOL_LAYERS, D), f32)
    conv_w_in = nrm(ks[7], (N_CONV_LAYERS, D, 3 * D), f32) * (D ** -0.5)
    conv_w = nrm(ks[8], (N_CONV_LAYERS, CONV_WIDTH, D), f32) * (CONV_WIDTH ** -0.5)
    conv_w_out = nrm(ks[9], (N_CONV_LAYERS, D, D), f32) * (D ** -0.5) * BETA
    router_w = nrm(ks[10], (D, N_EXPERTS), f32) * (D ** -0.5)
    router_b = 0.01 * nrm(ks[11], (N_EXPERTS,), f32)
    expert_w_gate = nrm(ks[12], (DEPTH, N_EXPERTS, D, D_EXPERT), f32) * (D ** -0.5)
    expert_w_up = nrm(ks[13], (DEPTH, N_EXPERTS, D, D_EXPERT), f32) * (D ** -0.5)
    expert_w_down = nrm(ks[14], (DEPTH, N_EXPERTS, D_EXPERT, D), f32) * (D_EXPERT ** -0.5) * BETA
    ln_g = 1.0 + 0.02 * nrm(ks[15], (DEPTH, 2, D), f32)
    ln_b = 0.02 * nrm(ks[16], (DEPTH, 2, D), f32)
    return {"x_prompt": x_prompt, "x_sample": x_sample,
            "attn_w_qkv": attn_w_qkv, "attn_w_o": attn_w_o, "attn_rpb": attn_rpb,
            "pool_w": pool_w, "pool_scale": pool_scale,
            "conv_w_in": conv_w_in, "conv_w": conv_w, "conv_w_out": conv_w_out,
            "router_w": router_w, "router_b": router_b,
            "expert_w_gate": expert_w_gate, "expert_w_up": expert_w_up, "expert_w_down": expert_w_down,
            "ln_g": ln_g, "ln_b": ln_b}


def reference(x_prompt, x_sample, attn_w_qkv, attn_w_o, attn_rpb, pool_w, pool_scale, conv_w_in, conv_w,
              conv_w_out, router_w, router_b, expert_w_gate, expert_w_up, expert_w_down, ln_g, ln_b):
    y_prompt = trunk(x_prompt, attn_w_qkv, attn_w_o, attn_rpb, pool_w, pool_scale, conv_w_in, conv_w, conv_w_out,
                     router_w, router_b, expert_w_gate, expert_w_up, expert_w_down, ln_g, ln_b)
    y_sample = trunk(x_sample, attn_w_qkv, attn_w_o, attn_rpb, pool_w, pool_scale, conv_w_in, conv_w, conv_w_out,
                     router_w, router_b, expert_w_gate, expert_w_up, expert_w_down, ln_g, ln_b)
    return (y_prompt, y_sample)
```

```python
import functools

import numpy as np
import jax
import jax.numpy as jnp
from jax import lax
from jax.experimental import pallas as pl
from jax.experimental.pallas import tpu as pltpu

F32 = jnp.float32
BF16 = jnp.bfloat16
I32 = jnp.int32

D_MODEL = 2048
DEPTH = 4
GRID_W = 64
N_MIXERS = 3
N_HEADS = 16
HEAD_DIM = D_MODEL // N_HEADS
WIN_ROWS = 8
WIN_COLS = 16
COL_BLOCK = 16
KEY_COLS = 2 * COL_BLOCK
N_COL_BLOCKS = GRID_W // COL_BLOCK
POOL_WINDOWS = (2, 4, 8, 16)
POOL_GROUP_DIM = D_MODEL // len(POOL_WINDOWS)
POOL_HALO = 8
CONV_HALO = 16
N_EXPERTS = 16
N_EXPERT_GROUPS = 4
EXPERTS_PER_GROUP = N_EXPERTS // N_EXPERT_GROUPS
D_EXPERT = D_MODEL // 2
ALPHA = (2 * DEPTH) ** 0.25
LN_EPS = 1e-5
MASK_VALUE = -1e30

MM_TM = 1024
MM_TN = 1024
ATT_ROWS = 8
ATT_TOK = ATT_ROWS * GRID_W
ATT_KROWS = 16
ATT_KSUB_ROWS = 4
ATT_KSUB_TOK = ATT_KSUB_ROWS * GRID_W
ATT_N_KSUB = ATT_KROWS // ATT_KSUB_ROWS
ATT_HEADS_PER_STEP = 2
POST_TM = 256
MOE_TM = 256
LN2_TM = 512
VMEM_LIMIT = 56 * 1024 * 1024


def _cparams(sem):
    return pltpu.CompilerParams(dimension_semantics=sem, vmem_limit_bytes=VMEM_LIMIT)


def _mm_kernel(x_ref, w_ref, o_ref, *, n_scaled, scale):
    acc = jnp.dot(x_ref[...], w_ref[...], preferred_element_type=F32)
    if n_scaled:
        acc = acc * jnp.where(pl.program_id(1) < n_scaled, F32(scale), F32(1.0))
    o_ref[...] = acc.astype(o_ref.dtype)


def _project(x, w, *, n_scaled=0, scale=1.0, name):
    m, k = x.shape
    n = w.shape[1]
    assert m % MM_TM == 0 and n % MM_TN == 0
    return pl.pallas_call(
        functools.partial(_mm_kernel, n_scaled=n_scaled, scale=scale),
        out_shape=jax.ShapeDtypeStruct((m, n), BF16),
        grid=(m // MM_TM, n // MM_TN),
        in_specs=[pl.BlockSpec((MM_TM, k), lambda i, j: (i, 0)),
                  pl.BlockSpec((k, MM_TN), lambda i, j: (0, j))],
        out_specs=pl.BlockSpec((MM_TM, MM_TN), lambda i, j: (i, j)),
        compiler_params=_cparams(("parallel", "arbitrary")),
        name=name,
    )(x, w)


def _key_col_start(j):
    return int(np.clip(j * COL_BLOCK - WIN_COLS // 2, 0, GRID_W - KEY_COLS))


def _attn_tables(seqs):
    blocks = []
    row0 = 0
    for batch, length in seqs:
        rows = length // GRID_W
        assert rows % ATT_ROWS == 0 and rows >= ATT_KROWS
        for _ in range(batch):
            for r0 in range(0, rows, ATT_ROWS):
                variant = 0 if r0 == 0 else (2 if r0 == rows - ATT_ROWS else 1)
                ks = row0 + int(np.clip(r0 - WIN_ROWS // 2, 0, rows - ATT_KROWS))
                blocks.append((variant, (row0 + r0) // ATT_ROWS, ks // ATT_KSUB_ROWS))
            row0 += rows
    blocks.sort(key=lambda b: b[0])
    var = np.array([b[0] for b in blocks], np.int32)
    gtab = np.array([b[1] for b in blocks], np.int32)
    kwin = np.array([b[2] for b in blocks], np.int32)
    return gtab, kwin, var


def _attn_bias(rpb):
    i = np.arange(ATT_ROWS)[:, None, None, None]
    cq = np.arange(COL_BLOCK)[None, :, None, None]
    rk = np.arange(ATT_KROWS)[None, None, :, None]
    ck = np.arange(KEY_COLS)[None, None, None, :]
    out = []
    for variant in range(3):
        if variant == 0:
            rel_start = np.maximum(i - WIN_ROWS // 2, 0)
            dy = rk - i + WIN_ROWS - 1
        elif variant == 1:
            rel_start = i
            dy = rk - i + WIN_ROWS - 1 - WIN_ROWS // 2
        else:
            rel_start = np.minimum(i + WIN_ROWS // 2, ATT_KROWS - WIN_ROWS)
            dy = rk - i + WIN_ROWS - 1 - (ATT_KROWS - ATT_ROWS)
        row_ok = (rk >= rel_start) & (rk < rel_start + WIN_ROWS)
        per_j = []
        for j in range(N_COL_BLOCKS):
            qc = j * COL_BLOCK + cq
            kc = _key_col_start(j) + ck
            c_start = np.clip(qc - WIN_COLS // 2, 0, GRID_W - WIN_COLS)
            col_ok = (kc >= c_start) & (kc < c_start + WIN_COLS)
            dx = np.clip(kc - qc + WIN_COLS - 1, 0, 2 * WIN_COLS - 2)
            ok = np.broadcast_to(row_ok & col_ok, (ATT_ROWS, COL_BLOCK, ATT_KROWS, KEY_COLS))
            dyb = np.broadcast_to(np.clip(dy, 0, 2 * WIN_ROWS - 2), ok.shape)
            dxb = np.broadcast_to(dx, ok.shape)
            nq, nk = ATT_ROWS * COL_BLOCK, ATT_KROWS * KEY_COLS
            b = rpb[:, dyb.reshape(nq, nk), dxb.reshape(nq, nk)]
            per_j.append(jnp.where(jnp.asarray(ok.reshape(nq, nk))[None], b.astype(F32), F32(MASK_VALUE)))
        out.append(jnp.stack(per_j, axis=1))
    return jnp.stack(out, axis=0)


def _attn_kernel(gtab, kwin, var, q_ref, k0, k1, k2, k3, v0, v1, v2, v3, bias_ref, o_ref, kf_ref, vf_ref):
    del gtab, kwin, var
    krefs = (k0, k1, k2, k3)
    vrefs = (v0, v1, v2, v3)
    nt = (((1,), (1,)), ((), ()))
    for hh in range(ATT_HEADS_PER_STEP):
        ls = slice(hh * HEAD_DIM, (hh + 1) * HEAD_DIM)
        for i in range(ATT_N_KSUB):
            kf_ref[i * ATT_KSUB_TOK:(i + 1) * ATT_KSUB_TOK, :] = krefs[i][:, ls].astype(F32)
            vf_ref[i * ATT_KSUB_TOK:(i + 1) * ATT_KSUB_TOK, :] = vrefs[i][:, ls].astype(F32)
        for j in range(N_COL_BLOCKS):
            kc0 = _key_col_start(j)
            qj = jnp.concatenate(
                [q_ref[rr * GRID_W + j * COL_BLOCK:rr * GRID_W + (j + 1) * COL_BLOCK, ls] for rr in range(ATT_ROWS)],
                axis=0)
            kj = jnp.concatenate(
                [kf_ref[rk * GRID_W + kc0:rk * GRID_W + kc0 + KEY_COLS, :] for rk in range(ATT_KROWS)],
                axis=0).astype(BF16)
            vj = jnp.concatenate(
                [vf_ref[rk * GRID_W + kc0:rk * GRID_W + kc0 + KEY_COLS, :] for rk in range(ATT_KROWS)],
                axis=0).astype(BF16)
            s = lax.dot_general(qj, kj, nt, preferred_element_type=F32) + bias_ref[hh, j]
            m = jnp.max(s, axis=-1, keepdims=True)
            p = jnp.exp(s - m)
            denom = jnp.sum(p, axis=-1, keepdims=True)
            o = jnp.dot(p.astype(BF16), vj, preferred_element_type=F32) / denom
            ob = o.astype(BF16)
            for rr in range(ATT_ROWS):
                o_ref[rr * GRID_W + j * COL_BLOCK:rr * GRID_W + (j + 1) * COL_BLOCK, ls] = \
                    ob[rr * COL_BLOCK:(rr + 1) * COL_BLOCK, :]


def _attention(qkv, bias, tables):
    n = qkv.shape[0]
    gtab, kwin, var = tables
    hw = ATT_HEADS_PER_STEP * HEAD_DIM
    n_hsteps = N_HEADS // ATT_HEADS_PER_STEP
    kv_specs = []
    for part in (1, 2):
        for i in range(ATT_N_KSUB):
            kv_specs.append(pl.BlockSpec(
                (ATT_KSUB_TOK, hw),
                lambda hp, s, gtab, kwin, var, i=i, part=part: (kwin[s] + i, part * n_hsteps + hp)))
    grid_spec = pltpu.PrefetchScalarGridSpec(
        num_scalar_prefetch=3,
        grid=(n_hsteps, len(gtab)),
        in_specs=[pl.BlockSpec((ATT_TOK, hw), lambda hp, s, gtab, kwin, var: (gtab[s], hp))] + kv_specs + [
            pl.BlockSpec((None, ATT_HEADS_PER_STEP, N_COL_BLOCKS, ATT_ROWS * COL_BLOCK, ATT_KROWS * KEY_COLS),
                         lambda hp, s, gtab, kwin, var: (var[s], hp, 0, 0, 0))],
        out_specs=pl.BlockSpec((ATT_TOK, hw), lambda hp, s, gtab, kwin, var: (gtab[s], hp)),
        scratch_shapes=[pltpu.VMEM((ATT_KROWS * GRID_W, HEAD_DIM), F32),
                        pltpu.VMEM((ATT_KROWS * GRID_W, HEAD_DIM), F32)],
    )
    return pl.pallas_call(
        _attn_kernel,
        out_shape=jax.ShapeDtypeStruct((n, D_MODEL), BF16),
        grid_spec=grid_spec,
        compiler_params=_cparams(("parallel", "arbitrary")),
        name="nbr_attention",
    )(jnp.asarray(gtab), jnp.asarray(kwin), jnp.asarray(var), qkv, *([qkv] * (2 * ATT_N_KSUB)), bias)


def _first_max4(vals):
    best = vals[0]
    idx = jnp.zeros(best.shape, I32)
    for i in range(1, 4):
        take = vals[i] > best
        best = jnp.where(take, vals[i], best)
        idx = jnp.where(take, I32(i), idx)
    return idx


def _select4(idx, vals):
    return jnp.where(idx == 0, vals[0], jnp.where(idx == 1, vals[1], jnp.where(idx == 2, vals[2], vals[3])))


def _ln_route(z, lng_ref, lnb_ref, rw2_ref, rwhi_ref, rb_ref, tri_ref, x1_ref, ri_ref, rg_ref, cnt_ref, carry_ref):
    tm = z.shape[0]

    @pl.when(pl.program_id(0) == 0)
    def _():
        carry_ref[...] = jnp.zeros_like(carry_ref)

    mu = jnp.mean(z, axis=-1, keepdims=True)
    zc = z - mu
    var = jnp.mean(zc * zc, axis=-1, keepdims=True)
    x1 = zc * lax.rsqrt(var + LN_EPS) * lng_ref[...] + lnb_ref[...]
    x1_ref[...] = x1

    hi = x1.astype(BF16)
    lo = (x1 - hi.astype(F32)).astype(BF16)
    nt = (((1,), (1,)), ((), ()))
    l2 = lax.dot_general(rw2_ref[...], hi, nt, preferred_element_type=F32)
    l1 = lax.dot_general(rwhi_ref[...], lo, nt, preferred_element_type=F32)
    logits = l2[:N_EXPERTS] + l2[N_EXPERTS:] + l1
    ex = jnp.exp(logits - jnp.max(logits, axis=0, keepdims=True))
    scores = ex / jnp.sum(ex, axis=0, keepdims=True)
    biased = scores + rb_ref[...]
    sr = [scores[e:e + 1, :] for e in range(N_EXPERTS)]
    br = [biased[e:e + 1, :] for e in range(N_EXPERTS)]

    group_scores = []
    for g in range(N_EXPERT_GROUPS):
        a, b, c, d = br[g * EXPERTS_PER_GROUP:(g + 1) * EXPERTS_PER_GROUP]
        hi1, lo1, hi2, lo2 = jnp.maximum(a, b), jnp.minimum(a, b), jnp.maximum(c, d), jnp.minimum(c, d)
        group_scores.append(jnp.maximum(hi1, hi2) + jnp.maximum(jnp.minimum(hi1, hi2), jnp.maximum(lo1, lo2)))
    gsel = _first_max4(group_scores)
    vb = [_select4(gsel, [br[g * EXPERTS_PER_GROUP + i] for g in range(N_EXPERT_GROUPS)]) for i in range(4)]
    vs = [_select4(gsel, [sr[g * EXPERTS_PER_GROUP + i] for g in range(N_EXPERT_GROUPS)]) for i in range(4)]
    i1 = _first_max4(vb)
    i2 = _first_max4([jnp.where(i1 == i, -jnp.inf, vb[i]) for i in range(4)])
    e0 = gsel * EXPERTS_PER_GROUP + i1
    e1 = gsel * EXPERTS_PER_GROUP + i2
    w0 = _select4(i1, vs)
    w1 = _select4(i2, vs)
    den = w0 + w1

    eio = lax.broadcasted_iota(I32, (N_EXPERTS, tm), 0)
    oh0 = eio == e0
    oh1 = eio == e1
    ohs = jnp.where(oh0 | oh1, F32(1.0), F32(0.0))
    carry = carry_ref[:, 0:1]
    before = jnp.dot(ohs.astype(BF16), tri_ref[...], preferred_element_type=F32) + carry
    r0 = jnp.sum(jnp.where(oh0, before, 0.0), axis=0, keepdims=True)
    r1 = jnp.sum(jnp.where(oh1, before, 0.0), axis=0, keepdims=True)
    carry_ref[...] = jnp.broadcast_to(carry + jnp.sum(ohs, axis=1, keepdims=True), carry_ref.shape)
    cnt_ref[...] = carry_ref[...]
    ri_ref[...] = jnp.concatenate([e0, e1, r0.astype(I32), r1.astype(I32), jnp.zeros((4, tm), I32)], axis=0)
    rg_ref[...] = jnp.concatenate([w0 / den, w1 / den, jnp.zeros((6, tm), F32)], axis=0)


def _seq_position(tm, seqs):
    (b0, l0), (_, l1) = seqs
    start = pl.program_id(0) * tm
    in_first = start < b0 * l0
    length = jnp.where(in_first, I32(l0), I32(l1))
    pos0 = jnp.where(in_first, start % l0, (start - b0 * l0) % l1)
    return pos0 + lax.broadcasted_iota(I32, (tm, 1), 0), length


def _post_attn_kernel(o_ref, w_ref, x_ref, *rest):
    y = jnp.dot(o_ref[...], w_ref[...], preferred_element_type=F32)
    _ln_route(ALPHA * x_ref[...] + y, *rest)


def _post_conv_kernel(b_ref, c_ref, h_ref, cp_ref, hp_ref, cn_ref, hn_ref, cw_ref, w_ref, x_ref, *rest, seqs):
    *route, u_ref = rest
    tm = b_ref.shape[0]
    pos, length = _seq_position(tm, seqs)
    h8 = POOL_HALO
    u_ref[0:h8, :] = (cp_ref[...].astype(F32) * hp_ref[...].astype(F32))[CONV_HALO - h8:, :]
    u_ref[h8:h8 + tm, :] = c_ref[...].astype(F32) * h_ref[...].astype(F32)
    u_ref[h8 + tm:, :] = (cn_ref[...].astype(F32) * hn_ref[...].astype(F32))[0:h8, :]
    prev = jnp.where(pos >= 1, u_ref[h8 - 1:h8 - 1 + tm, :], 0.0)
    nxt = jnp.where(pos + 1 < length, u_ref[h8 + 1:h8 + 1 + tm, :], 0.0)
    conv = prev * cw_ref[0:1, :] + u_ref[h8:h8 + tm, :] * cw_ref[1:2, :] + nxt * cw_ref[2:3, :]
    zin = (b_ref[...].astype(F32) * conv).astype(BF16)
    y = jnp.dot(zin, w_ref[...], preferred_element_type=F32)
    _ln_route(ALPHA * x_ref[...] + y, *route)


def _post_pool_kernel(xp_ref, xn_ref, w_ref, sc_ref, x_ref, *rest, seqs):
    *route, xs_ref = rest
    tm = x_ref.shape[0]
    pos, length = _seq_position(tm, seqs)
    h8 = POOL_HALO
    xs_ref[0:h8, :] = xp_ref[...]
    xs_ref[h8:h8 + tm, :] = x_ref[...]
    xs_ref[h8 + tm:, :] = xn_ref[...]
    ys = []
    for g, w in enumerate(POOL_WINDOWS):
        cs = slice(g * POOL_GROUP_DIM, (g + 1) * POOL_GROUP_DIM)
        acc = jnp.zeros((tm, POOL_GROUP_DIM), F32)
        for s in range(-(w // 2), w - w // 2):
            ok = (pos + s >= 0) & (pos + s < length)
            acc = acc + jnp.where(ok, xs_ref[h8 + s:h8 + s + tm, cs], 0.0)
        lo = jnp.maximum(pos - w // 2, 0)
        hi = jnp.minimum(pos - w // 2 + w, length)
        pooled = acc / (hi - lo).astype(F32) - x_ref[:, cs]
        ys.append(jnp.dot(pooled.astype(BF16), w_ref[g], preferred_element_type=F32))
    y = jnp.concatenate(ys, axis=1) * sc_ref[...]
    _ln_route(ALPHA * x_ref[...] + y, *route)


def _route_operands(router_w, router_b, tm):
    rw_t = router_w.T
    rw_hi = rw_t.astype(BF16)
    rw_lo = (rw_t - rw_hi.astype(F32)).astype(BF16)
    tri = jnp.asarray(np.triu(np.ones((tm, tm), np.float32), 1), BF16)
    return jnp.concatenate([rw_hi, rw_lo], axis=0), rw_hi, router_b.reshape(N_EXPERTS, 1).astype(F32), tri


def _post_call(kind, n, mixer_args, mixer_specs, x, lng, lnb, route_ops, scratch, seqs):
    tm = POST_TM
    rw2, rwhi, rb, tri = route_ops
    const = lambda shape: pl.BlockSpec(shape, lambda i: (0,) * len(shape))
    in_specs = list(mixer_specs) + [
        pl.BlockSpec((tm, D_MODEL), lambda i: (i, 0)),
        const((1, D_MODEL)), const((1, D_MODEL)),
        const((2 * N_EXPERTS, D_MODEL)), const((N_EXPERTS, D_MODEL)), const((N_EXPERTS, 1)), const((tm, tm))]
    out_shape = [jax.ShapeDtypeStruct((n, D_MODEL), F32),
                 jax.ShapeDtypeStruct((8, n), I32),
                 jax.ShapeDtypeStruct((8, n), F32),
                 jax.ShapeDtypeStruct((N_EXPERTS, 128), F32)]
    out_specs = [pl.BlockSpec((tm, D_MODEL), lambda i: (i, 0)),
                 pl.BlockSpec((8, tm), lambda i: (0, i)),
                 pl.BlockSpec((8, tm), lambda i: (0, i)),
                 const((N_EXPERTS, 128))]
    body = {"attn": _post_attn_kernel,
            "conv": functools.partial(_post_conv_kernel, seqs=seqs),
            "pool": functools.partial(_post_pool_kernel, seqs=seqs)}[kind]
    return pl.pallas_call(
        body,
        out_shape=out_shape,
        grid=(n // tm,),
        in_specs=in_specs,
        out_specs=out_specs,
        scratch_shapes=[pltpu.VMEM((N_EXPERTS, 128), F32)] + list(scratch),
        compiler_params=_cparams(("arbitrary",)),
        name="post_" + kind,
    )(*mixer_args, x, lng, lnb, rw2, rwhi, rb, tri)


def _moe_kernel(bexp, nvalid, inv, x_hbm, wg_ref, wu_ref, wd_ref, out_hbm, xbuf, ybuf, gsem, ssem):
    del bexp
    tm = ybuf.shape[0]
    b = pl.program_id(0)
    nb = pl.num_programs(0)
    slot = b % 2

    def gather_copy(row, tok, sl):
        return pltpu.make_async_copy(x_hbm.at[pl.ds(tok, 1)], xbuf.at[sl, pl.ds(row, 1)], gsem.at[sl])

    def scatter_copy(row, code):
        return pltpu.make_async_copy(ybuf.at[pl.ds(row, 1)], out_hbm.at[code & 1, pl.ds(code >> 1, 1)], ssem.at[0])

    def start_gather(blk, sl):
        def body(r, carry):
            gather_copy(r, inv[blk * tm + r] >> 1, sl).start()
            return carry
        lax.fori_loop(0, nvalid[blk], body, 0)

    def wait_scatter(blk):
        def body(r, carry):
            scatter_copy(r, inv[blk * tm + r]).wait()
            return carry
        lax.fori_loop(0, nvalid[blk], body, 0)

    @pl.when(b == 0)
    def _():
        xbuf[...] = jnp.zeros_like(xbuf)
        start_gather(0, 0)

    @pl.when(b + 1 < nb)
    def _():
        start_gather(b + 1, 1 - slot)

    n = nvalid[b]

    def wait_gather(r, carry):
        gather_copy(r, inv[b * tm + r] >> 1, slot).wait()
        return carry
    lax.fori_loop(0, n, wait_gather, 0)

    @pl.when(n > 0)
    def _():
        x = xbuf[slot].astype(BF16)
        gate = jnp.dot(x, wg_ref[...], preferred_element_type=F32)
        up = jnp.dot(x, wu_ref[...], preferred_element_type=F32)
        hdn = (gate * jax.nn.sigmoid(gate) * up).astype(BF16)
        y = jnp.dot(hdn, wd_ref[...], preferred_element_type=F32)

        @pl.when(b > 0)
        def _():
            wait_scatter(b - 1)
        ybuf[...] = y

        def body(r, carry):
            scatter_copy(r, inv[b * tm + r]).start()
            return carry
        lax.fori_loop(0, n, body, 0)

    @pl.when((n == 0) & (b > 0))
    def _():
        wait_scatter(b - 1)

    @pl.when(b == nb - 1)
    def _():
        wait_scatter(b)


def _moe(x1, wg, wu, wd, bexp, nvalid, inv, n_blocks):
    n = x1.shape[0]
    tm = MOE_TM
    wspec = lambda shape: pl.BlockSpec((None,) + shape, lambda b, bexp, nvalid, inv: (bexp[b], 0, 0))
    grid_spec = pltpu.PrefetchScalarGridSpec(
        num_scalar_prefetch=3,
        grid=(n_blocks,),
        in_specs=[pl.BlockSpec(memory_space=pl.ANY),
                  wspec((D_MODEL, D_EXPERT)), wspec((D_MODEL, D_EXPERT)), wspec((D_EXPERT, D_MODEL))],
        out_specs=pl.BlockSpec(memory_space=pl.ANY),
        scratch_shapes=[pltpu.VMEM((2, tm, D_MODEL), F32), pltpu.VMEM((tm, D_MODEL), F32),
                        pltpu.SemaphoreType.DMA((2,)), pltpu.SemaphoreType.DMA((1,))],
    )
    return pl.pallas_call(
        _moe_kernel,
        out_shape=jax.ShapeDtypeStruct((2, n, D_MODEL), F32),
        grid_spec=grid_spec,
        compiler_params=_cparams(("arbitrary",)),
        name="moe_experts",
    )(bexp, nvalid, inv, x1, wg, wu, wd)


def _dispatch_tables(ri, cnt, n):
    tm = MOE_TM
    n_blocks = -(-2 * n // tm) + N_EXPERTS
    experts = ri[0:2]
    rank = ri[2:4]
    counts = cnt[:, 0].astype(I32)
    nblk = (counts + tm - 1) // tm
    bend = jnp.cumsum(nblk)
    bstart = bend - nblk
    dest = bstart[experts] * tm + rank
    code = 2 * jnp.arange(n, dtype=I32)[None, :] + jnp.arange(2, dtype=I32)[:, None]
    inv = jnp.zeros((n_blocks * tm,), I32).at[dest.reshape(-1)].set(code.reshape(-1), unique_indices=True)
    blocks = jnp.arange(n_blocks, dtype=I32)
    used = bend[-1]
    bexp_raw = jnp.minimum(jnp.searchsorted(bend, blocks, side="right"), N_EXPERTS - 1).astype(I32)
    bexp = bexp_raw[jnp.minimum(blocks, used - 1)]
    nvalid = jnp.where(blocks < used, jnp.clip(counts[bexp] - (blocks - bstart[bexp]) * tm, 0, tm), 0).astype(I32)
    return bexp, nvalid, inv, n_blocks


def _ln2_kernel(x_ref, y0_ref, y1_ref, g_ref, lng_ref, lnb_ref, o_ref, ob_ref):
    z = ALPHA * x_ref[...] + y0_ref[...] * g_ref[:, 0:1] + y1_ref[...] * g_ref[:, 1:2]
    mu = jnp.mean(z, axis=-1, keepdims=True)
    zc = z - mu
    var = jnp.mean(zc * zc, axis=-1, keepdims=True)
    out = zc * lax.rsqrt(var + LN_EPS) * lng_ref[...] + lnb_ref[...]
    o_ref[...] = out
    ob_ref[...] = out.astype(BF16)


def _combine_ln(x1, y2, gates, lng, lnb):
    n = x1.shape[0]
    tm = LN2_TM
    row = lambda i: (i, 0)
    const = lambda i: (0, 0)
    return pl.pallas_call(
        _ln2_kernel,
        out_shape=[jax.ShapeDtypeStruct((n, D_MODEL), F32), jax.ShapeDtypeStruct((n, D_MODEL), BF16)],
        grid=(n // tm,),
        in_specs=[pl.BlockSpec((tm, D_MODEL), row),
                  pl.BlockSpec((None, tm, D_MODEL), lambda i: (0, i, 0)),
                  pl.BlockSpec((None, tm, D_MODEL), lambda i: (1, i, 0)),
                  pl.BlockSpec((tm, 2), row),
                  pl.BlockSpec((1, D_MODEL), const), pl.BlockSpec((1, D_MODEL), const)],
        out_specs=[pl.BlockSpec((tm, D_MODEL), row), pl.BlockSpec((tm, D_MODEL), row)],
        compiler_params=_cparams(("parallel",)),
        name="combine_ln",
    )(x1, y2, y2, gates, lng, lnb)


def _trunk(x, seqs, attn_w_qkv, attn_w_o, attn_rpb, pool_w, pool_scale, conv_w_in, conv_w, conv_w_out,
           router_w, router_b, expert_w_gate, expert_w_up, expert_w_down, ln_g, ln_b):
    n = x.shape[0]
    tm = POST_TM
    assert n % MM_TM == 0 and n % LN2_TM == 0 and all(l % tm == 0 for _, l in seqs)
    route_ops = _route_operands(router_w, router_b, tm)
    attn_tables = _attn_tables(seqs)
    xb = x.astype(BF16)
    row = lambda i: (i, 0)
    const2 = lambda i: (0, 0)
    for layer in range(DEPTH):
        kind, slot = layer % N_MIXERS, layer // N_MIXERS
        lng1, lnb1 = ln_g[layer, 0].reshape(1, D_MODEL), ln_b[layer, 0].reshape(1, D_MODEL)
        if kind == 0:
            qkv = _project(xb, attn_w_qkv[slot].astype(BF16), n_scaled=D_MODEL // MM_TN, scale=HEAD_DIM ** -0.5,
                           name="qkv_proj")
            o = _attention(qkv, _attn_bias(attn_rpb[slot]), attn_tables)
            args = (o, attn_w_o[slot].astype(BF16))
            specs = (pl.BlockSpec((tm, D_MODEL), row), pl.BlockSpec((D_MODEL, D_MODEL), const2))
            post = _post_call("attn", n, args, specs, x, lng1, lnb1, route_ops, (), seqs)
        elif kind == 1:
            nb8 = n // POOL_HALO
            args = (x, x, pool_w[slot].astype(BF16), pool_scale[slot].reshape(1, D_MODEL))
            specs = (pl.BlockSpec((POOL_HALO, D_MODEL), lambda i: (jnp.maximum(i * (tm // POOL_HALO) - 1, 0), 0)),
                     pl.BlockSpec((POOL_HALO, D_MODEL),
                                  lambda i: (jnp.minimum((i + 1) * (tm // POOL_HALO), nb8 - 1), 0)),
                     pl.BlockSpec((len(POOL_WINDOWS), POOL_GROUP_DIM, POOL_GROUP_DIM), lambda i: (0, 0, 0)),
                     pl.BlockSpec((1, D_MODEL), const2))
            scratch = (pltpu.VMEM((tm + 2 * POOL_HALO, D_MODEL), F32),)
            post = _post_call("pool", n, args, specs, x, lng1, lnb1, route_ops, scratch, seqs)
        else:
            bch = _project(xb, conv_w_in[slot].astype(BF16), name="conv_in_proj")
            nb16 = n // CONV_HALO
            prev = lambda c: pl.BlockSpec((CONV_HALO, D_MODEL),
                                          lambda i, c=c: (jnp.maximum(i * (tm // CONV_HALO) - 1, 0), c))
            nxt = lambda c: pl.BlockSpec((CONV_HALO, D_MODEL),
                                         lambda i, c=c: (jnp.minimum((i + 1) * (tm // CONV_HALO), nb16 - 1), c))
            args = (bch,) * 7 + (conv_w[slot], conv_w_out[slot].astype(BF16))
            specs = (pl.BlockSpec((tm, D_MODEL), lambda i: (i, 0)),
                     pl.BlockSpec((tm, D_MODEL), lambda i: (i, 1)),
                     pl.BlockSpec((tm, D_MODEL), lambda i: (i, 2)),
                     prev(1), prev(2), nxt(1), nxt(2),
                     pl.BlockSpec((3, D_MODEL), const2),
                     pl.BlockSpec((D_MODEL, D_MODEL), const2))
            scratch = (pltpu.VMEM((tm + 2 * POOL_HALO, D_MODEL), F32),)
            post = _post_call("conv", n, args, specs, x, lng1, lnb1, route_ops, scratch, seqs)
        x1, ri, rg, cnt = post
        bexp, nvalid, inv, n_blocks = _dispatch_tables(ri, cnt, n)
        y2 = _moe(x1, expert_w_gate[layer].astype(BF16), expert_w_up[layer].astype(BF16),
                  expert_w_down[layer].astype(BF16), bexp, nvalid, inv, n_blocks)
        x, xb = _combine_ln(x1, y2, rg[0:2].T, ln_g[layer, 1].reshape(1, D_MODEL), ln_b[layer, 1].reshape(1, D_MODEL))
    return x


def kernel(x_prompt, x_sample, attn_w_qkv, attn_w_o, attn_rpb, pool_w, pool_scale, conv_w_in, conv_w, conv_w_out,
           router_w, router_b, expert_w_gate, expert_w_up, expert_w_down, ln_g, ln_b):
    seqs = (x_prompt.shape[:2], x_sample.shape[:2])
    n0 = x_prompt.shape[0] * x_prompt.shape[1]
    x = jnp.concatenate([x_prompt.reshape(-1, D_MODEL), x_sample.reshape(-1, D_MODEL)], axis=0)
    y = _trunk(x, seqs, attn_w_qkv, attn_w_o, attn_rpb, pool_w, pool_scale, conv_w_in, conv_w, conv_w_out,
               router_w, router_b, expert_w_gate, expert_w_up, expert_w_down, ln_g, ln_b)
    return y[:n0].reshape(x_prompt.shape), y[n0:].reshape(x_sample.shape)
```

```python
import functools

import numpy as np
import jax
import jax.numpy as jnp
from jax import lax
from jax.experimental import pallas as pl
from jax.experimental.pallas import tpu as pltpu

F32 = jnp.float32
BF16 = jnp.bfloat16
I32 = jnp.int32

D_MODEL = 2048
DEPTH = 4
GRID_W = 64
N_MIXERS = 3
N_HEADS = 16
HEAD_DIM = D_MODEL // N_HEADS
WIN_ROWS = 8
WIN_COLS = 16
COL_BLOCK = 16
KEY_COLS = 2 * COL_BLOCK
N_COL_BLOCKS = GRID_W // COL_BLOCK
POOL_WINDOWS = (2, 4, 8, 16)
POOL_GROUP_DIM = D_MODEL // len(POOL_WINDOWS)
POOL_HALO = 8
CONV_HALO = 16
N_EXPERTS = 16
N_EXPERT_GROUPS = 4
EXPERTS_PER_GROUP = N_EXPERTS // N_EXPERT_GROUPS
D_EXPERT = D_MODEL // 2
ALPHA = (2 * DEPTH) ** 0.25
LN_EPS = 1e-5
MASK_VALUE = -1e30

MM_TM = 1024
MM_TN = 1024
ATT_ROWS = 8
ATT_TOK = ATT_ROWS * GRID_W
ATT_KROWS = 16
ATT_KSUB_ROWS = 4
ATT_KSUB_TOK = ATT_KSUB_ROWS * GRID_W
ATT_N_KSUB = ATT_KROWS // ATT_KSUB_ROWS
ATT_HEADS_PER_STEP = 2
POST_TM = 256
MOE_TM = 256
MOE_LEAD_BLOCKS = 2
LN2_TM = 512
VMEM_LIMIT = 56 * 1024 * 1024


def _cparams(sem):
    return pltpu.CompilerParams(dimension_semantics=sem, vmem_limit_bytes=VMEM_LIMIT)


def _mm_kernel(x_ref, w_ref, o_ref, *, n_scaled, scale):
    acc = jnp.dot(x_ref[...], w_ref[...], preferred_element_type=F32)
    if n_scaled:
        acc = acc * jnp.where(pl.program_id(1) < n_scaled, F32(scale), F32(1.0))
    o_ref[...] = acc.astype(o_ref.dtype)


def _project(x, w, *, n_scaled=0, scale=1.0, name):
    m, k = x.shape
    n = w.shape[1]
    assert m % MM_TM == 0 and n % MM_TN == 0
    return pl.pallas_call(
        functools.partial(_mm_kernel, n_scaled=n_scaled, scale=scale),
        out_shape=jax.ShapeDtypeStruct((m, n), BF16),
        grid=(m // MM_TM, n // MM_TN),
        in_specs=[pl.BlockSpec((MM_TM, k), lambda i, j: (i, 0)),
                  pl.BlockSpec((k, MM_TN), lambda i, j: (0, j))],
        out_specs=pl.BlockSpec((MM_TM, MM_TN), lambda i, j: (i, j)),
        compiler_params=_cparams(("parallel", "arbitrary")),
        name=name,
    )(x, w)


def _key_col_start(j):
    return int(np.clip(j * COL_BLOCK - WIN_COLS // 2, 0, GRID_W - KEY_COLS))


def _attn_tables(seqs):
    blocks = []
    row0 = 0
    for batch, length in seqs:
        rows = length // GRID_W
        assert rows % ATT_ROWS == 0 and rows >= ATT_KROWS
        for _ in range(batch):
            for r0 in range(0, rows, ATT_ROWS):
                variant = 0 if r0 == 0 else (2 if r0 == rows - ATT_ROWS else 1)
                ks = row0 + int(np.clip(r0 - WIN_ROWS // 2, 0, rows - ATT_KROWS))
                blocks.append((variant, (row0 + r0) // ATT_ROWS, ks // ATT_KSUB_ROWS))
            row0 += rows
    blocks.sort(key=lambda b: b[0])
    var = np.array([b[0] for b in blocks], np.int32)
    gtab = np.array([b[1] for b in blocks], np.int32)
    kwin = np.array([b[2] for b in blocks], np.int32)
    return gtab, kwin, var


def _attn_bias(rpb):
    i = np.arange(ATT_ROWS)[:, None, None, None]
    cq = np.arange(COL_BLOCK)[None, :, None, None]
    rk = np.arange(ATT_KROWS)[None, None, :, None]
    ck = np.arange(KEY_COLS)[None, None, None, :]
    n_dy, n_dx = 2 * WIN_ROWS - 1, 2 * WIN_COLS - 1
    sel_dy = np.zeros((3, n_dy, ATT_ROWS, ATT_KROWS), np.float32)
    row_ok = np.zeros((3, ATT_ROWS, ATT_KROWS), bool)
    for variant in range(3):
        if variant == 0:
            rel_start = np.maximum(i - WIN_ROWS // 2, 0)
            dy = rk - i + WIN_ROWS - 1
        elif variant == 1:
            rel_start = i
            dy = rk - i + WIN_ROWS - 1 - WIN_ROWS // 2
        else:
            rel_start = np.minimum(i + WIN_ROWS // 2, ATT_KROWS - WIN_ROWS)
            dy = rk - i + WIN_ROWS - 1 - (ATT_KROWS - ATT_ROWS)
        row_ok[variant] = ((rk >= rel_start) & (rk < rel_start + WIN_ROWS))[:, 0, :, 0]
        dyc = np.clip(dy, 0, n_dy - 1)[:, 0, :, 0]
        sel_dy[variant] = np.arange(n_dy)[:, None, None] == dyc[None]
    sel_dx = np.zeros((N_COL_BLOCKS, n_dx, COL_BLOCK, KEY_COLS), np.float32)
    col_ok = np.zeros((N_COL_BLOCKS, COL_BLOCK, KEY_COLS), bool)
    for j in range(N_COL_BLOCKS):
        qc = j * COL_BLOCK + cq
        kc = _key_col_start(j) + ck
        c_start = np.clip(qc - WIN_COLS // 2, 0, GRID_W - WIN_COLS)
        col_ok[j] = ((kc >= c_start) & (kc < c_start + WIN_COLS))[0, :, 0, :]
        dxc = np.clip(kc - qc + WIN_COLS - 1, 0, n_dx - 1)[0, :, 0, :]
        sel_dx[j] = np.arange(n_dx)[:, None, None] == dxc[None]
    hp = lax.Precision.HIGHEST
    by = jnp.einsum("hab,vair->vhirb", rpb.astype(F32), jnp.asarray(sel_dy), precision=hp)
    bias = jnp.einsum("vhirb,jbck->vhjicrk", by, jnp.asarray(sel_dx), precision=hp)
    ok = row_ok[:, None, None, :, None, :, None] & col_ok[None, None, :, None, :, None, :]
    bias = jnp.where(jnp.asarray(ok), bias, F32(MASK_VALUE))
    return bias.reshape(3, N_HEADS, N_COL_BLOCKS, ATT_ROWS * COL_BLOCK, ATT_KROWS * KEY_COLS)


def _attn_kernel(gtab, kwin, var, q_ref, k0, k1, k2, k3, v0, v1, v2, v3, bias_ref, o_ref, kf_ref, vf_ref):
    del gtab, kwin, var
    krefs = (k0, k1, k2, k3)
    vrefs = (v0, v1, v2, v3)
    nt = (((1,), (1,)), ((), ()))
    for hh in range(ATT_HEADS_PER_STEP):
        ls = slice(hh * HEAD_DIM, (hh + 1) * HEAD_DIM)
        for i in range(ATT_N_KSUB):
            kf_ref[i * ATT_KSUB_TOK:(i + 1) * ATT_KSUB_TOK, :] = krefs[i][:, ls].astype(F32)
            vf_ref[i * ATT_KSUB_TOK:(i + 1) * ATT_KSUB_TOK, :] = vrefs[i][:, ls].astype(F32)
        for j in range(N_COL_BLOCKS):
            kc0 = _key_col_start(j)
            qj = jnp.concatenate(
                [q_ref[rr * GRID_W + j * COL_BLOCK:rr * GRID_W + (j + 1) * COL_BLOCK, ls] for rr in range(ATT_ROWS)],
                axis=0)
            kj = jnp.concatenate(
                [kf_ref[rk * GRID_W + kc0:rk * GRID_W + kc0 + KEY_COLS, :] for rk in range(ATT_KROWS)],
                axis=0).astype(BF16)
            vj = jnp.concatenate(
                [vf_ref[rk * GRID_W + kc0:rk * GRID_W + kc0 + KEY_COLS, :] for rk in range(ATT_KROWS)],
                axis=0).astype(BF16)
            s = lax.dot_general(qj, kj, nt, preferred_element_type=F32) + bias_ref[hh, j]
            m = jnp.max(s, axis=-1, keepdims=True)
            p = jnp.exp(s - m)
            denom = jnp.sum(p, axis=-1, keepdims=True)
            o = jnp.dot(p.astype(BF16), vj, preferred_element_type=F32) / denom
            ob = o.astype(BF16)
            for rr in range(ATT_ROWS):
                o_ref[rr * GRID_W + j * COL_BLOCK:rr * GRID_W + (j + 1) * COL_BLOCK, ls] = \
                    ob[rr * COL_BLOCK:(rr + 1) * COL_BLOCK, :]


def _attention(qkv, bias, tables):
    n = qkv.shape[0]
    gtab, kwin, var = tables
    hw = ATT_HEADS_PER_STEP * HEAD_DIM
    n_hsteps = N_HEADS // ATT_HEADS_PER_STEP
    kv_specs = []
    for part in (1, 2):
        for i in range(ATT_N_KSUB):
            kv_specs.append(pl.BlockSpec(
                (ATT_KSUB_TOK, hw),
                lambda hp, s, gtab, kwin, var, i=i, part=part: (kwin[s] + i, part * n_hsteps + hp)))
    grid_spec = pltpu.PrefetchScalarGridSpec(
        num_scalar_prefetch=3,
        grid=(n_hsteps, len(gtab)),
        in_specs=[pl.BlockSpec((ATT_TOK, hw), lambda hp, s, gtab, kwin, var: (gtab[s], hp))] + kv_specs + [
            pl.BlockSpec((None, ATT_HEADS_PER_STEP, N_COL_BLOCKS, ATT_ROWS * COL_BLOCK, ATT_KROWS * KEY_COLS),
                         lambda hp, s, gtab, kwin, var: (var[s], hp, 0, 0, 0))],
        out_specs=pl.BlockSpec((ATT_TOK, hw), lambda hp, s, gtab, kwin, var: (gtab[s], hp)),
        scratch_shapes=[pltpu.VMEM((ATT_KROWS * GRID_W, HEAD_DIM), F32),
                        pltpu.VMEM((ATT_KROWS * GRID_W, HEAD_DIM), F32)],
    )
    return pl.pallas_call(
        _attn_kernel,
        out_shape=jax.ShapeDtypeStruct((n, D_MODEL), BF16),
        grid_spec=grid_spec,
        compiler_params=_cparams(("parallel", "arbitrary")),
        name="nbr_attention",
    )(jnp.asarray(gtab), jnp.asarray(kwin), jnp.asarray(var), qkv, *([qkv] * (2 * ATT_N_KSUB)), bias)


def _first_max4(vals):
    best = vals[0]
    idx = jnp.zeros(best.shape, I32)
    for i in range(1, 4):
        take = vals[i] > best
        best = jnp.where(take, vals[i], best)
        idx = jnp.where(take, I32(i), idx)
    return idx


def _select4(idx, vals):
    return jnp.where(idx == 0, vals[0], jnp.where(idx == 1, vals[1], jnp.where(idx == 2, vals[2], vals[3])))


def _ln_route(z, lng_ref, lnb_ref, rw2_ref, rwhi_ref, rb_ref, tri_ref, x1_ref, ri_ref, rg_ref, cnt_ref, carry_ref):
    tm = z.shape[0]

    @pl.when(pl.program_id(0) == 0)
    def _():
        carry_ref[...] = jnp.zeros_like(carry_ref)

    mu = jnp.mean(z, axis=-1, keepdims=True)
    zc = z - mu
    var = jnp.mean(zc * zc, axis=-1, keepdims=True)
    x1 = zc * lax.rsqrt(var + LN_EPS) * lng_ref[...] + lnb_ref[...]
    x1_ref[...] = x1

    hi = x1.astype(BF16)
    lo = (x1 - hi.astype(F32)).astype(BF16)
    nt = (((1,), (1,)), ((), ()))
    l2 = lax.dot_general(rw2_ref[...], hi, nt, preferred_element_type=F32)
    l1 = lax.dot_general(rwhi_ref[...], lo, nt, preferred_element_type=F32)
    logits = l2[:N_EXPERTS] + l2[N_EXPERTS:] + l1
    ex = jnp.exp(logits - jnp.max(logits, axis=0, keepdims=True))
    scores = ex / jnp.sum(ex, axis=0, keepdims=True)
    biased = scores + rb_ref[...]
    sr = [scores[e:e + 1, :] for e in range(N_EXPERTS)]
    br = [biased[e:e + 1, :] for e in range(N_EXPERTS)]

    group_scores = []
    for g in range(N_EXPERT_GROUPS):
        a, b, c, d = br[g * EXPERTS_PER_GROUP:(g + 1) * EXPERTS_PER_GROUP]
        hi1, lo1, hi2, lo2 = jnp.maximum(a, b), jnp.minimum(a, b), jnp.maximum(c, d), jnp.minimum(c, d)
        group_scores.append(jnp.maximum(hi1, hi2) + jnp.maximum(jnp.minimum(hi1, hi2), jnp.maximum(lo1, lo2)))
    gsel = _first_max4(group_scores)
    vb = [_select4(gsel, [br[g * EXPERTS_PER_GROUP + i] for g in range(N_EXPERT_GROUPS)]) for i in range(4)]
    vs = [_select4(gsel, [sr[g * EXPERTS_PER_GROUP + i] for g in range(N_EXPERT_GROUPS)]) for i in range(4)]
    i1 = _first_max4(vb)
    i2 = _first_max4([jnp.where(i1 == i, -jnp.inf, vb[i]) for i in range(4)])
    e0 = gsel * EXPERTS_PER_GROUP + i1
    e1 = gsel * EXPERTS_PER_GROUP + i2
    w0 = _select4(i1, vs)
    w1 = _select4(i2, vs)
    den = w0 + w1

    eio = lax.broadcasted_iota(I32, (N_EXPERTS, tm), 0)
    oh0 = eio == e0
    oh1 = eio == e1
    ohs = jnp.where(oh0 | oh1, F32(1.0), F32(0.0))
    carry = carry_ref[:, 0:1]
    before = jnp.dot(ohs.astype(BF16), tri_ref[...], preferred_element_type=F32) + carry
    r0 = jnp.sum(jnp.where(oh0, before, 0.0), axis=0, keepdims=True)
    r1 = jnp.sum(jnp.where(oh1, before, 0.0), axis=0, keepdims=True)
    carry_ref[...] = jnp.broadcast_to(carry + jnp.sum(ohs, axis=1, keepdims=True), carry_ref.shape)
    cnt_ref[...] = carry_ref[...]
    ri_ref[...] = jnp.concatenate([e0, e1, r0.astype(I32), r1.astype(I32), jnp.zeros((4, tm), I32)], axis=0)
    rg_ref[...] = jnp.concatenate([w0 / den, w1 / den, jnp.zeros((6, tm), F32)], axis=0)


def _seq_position(tm, seqs):
    (b0, l0), (_, l1) = seqs
    start = pl.program_id(0) * tm
    in_first = start < b0 * l0
    length = jnp.where(in_first, I32(l0), I32(l1))
    pos0 = jnp.where(in_first, start % l0, (start - b0 * l0) % l1)
    return pos0 + lax.broadcasted_iota(I32, (tm, 1), 0), length


def _post_attn_kernel(o_ref, w_ref, x_ref, *rest):
    y = jnp.dot(o_ref[...], w_ref[...], preferred_element_type=F32)
    _ln_route(ALPHA * x_ref[...] + y, *rest)


def _post_conv_kernel(b_ref, c_ref, h_ref, cp_ref, hp_ref, cn_ref, hn_ref, cw_ref, w_ref, x_ref, *rest, seqs):
    *route, u_ref = rest
    tm = b_ref.shape[0]
    pos, length = _seq_position(tm, seqs)
    h8 = POOL_HALO
    u_ref[0:h8, :] = (cp_ref[...].astype(F32) * hp_ref[...].astype(F32))[CONV_HALO - h8:, :]
    u_ref[h8:h8 + tm, :] = c_ref[...].astype(F32) * h_ref[...].astype(F32)
    u_ref[h8 + tm:, :] = (cn_ref[...].astype(F32) * hn_ref[...].astype(F32))[0:h8, :]
    prev = jnp.where(pos >= 1, u_ref[h8 - 1:h8 - 1 + tm, :], 0.0)
    nxt = jnp.where(pos + 1 < length, u_ref[h8 + 1:h8 + 1 + tm, :], 0.0)
    conv = prev * cw_ref[0:1, :] + u_ref[h8:h8 + tm, :] * cw_ref[1:2, :] + nxt * cw_ref[2:3, :]
    zin = (b_ref[...].astype(F32) * conv).astype(BF16)
    y = jnp.dot(zin, w_ref[...], preferred_element_type=F32)
    _ln_route(ALPHA * x_ref[...] + y, *route)


def _post_pool_kernel(xp_ref, xn_ref, w_ref, sc_ref, x_ref, *rest, seqs):
    *route, xs_ref = rest
    tm = x_ref.shape[0]
    pos, length = _seq_position(tm, seqs)
    h8 = POOL_HALO
    xs_ref[0:h8, :] = xp_ref[...]
    xs_ref[h8:h8 + tm, :] = x_ref[...]
    xs_ref[h8 + tm:, :] = xn_ref[...]
    ys = []
    for g, w in enumerate(POOL_WINDOWS):
        cs = slice(g * POOL_GROUP_DIM, (g + 1) * POOL_GROUP_DIM)
        acc = jnp.zeros((tm, POOL_GROUP_DIM), F32)
        for s in range(-(w // 2), w - w // 2):
            ok = (pos + s >= 0) & (pos + s < length)
            acc = acc + jnp.where(ok, xs_ref[h8 + s:h8 + s + tm, cs], 0.0)
        lo = jnp.maximum(pos - w // 2, 0)
        hi = jnp.minimum(pos - w // 2 + w, length)
        pooled = acc / (hi - lo).astype(F32) - x_ref[:, cs]
        ys.append(jnp.dot(pooled.astype(BF16), w_ref[g], preferred_element_type=F32))
    y = jnp.concatenate(ys, axis=1) * sc_ref[...]
    _ln_route(ALPHA * x_ref[...] + y, *route)


def _route_operands(router_w, router_b, tm):
    rw_t = router_w.T
    rw_hi = rw_t.astype(BF16)
    rw_lo = (rw_t - rw_hi.astype(F32)).astype(BF16)
    tri = jnp.asarray(np.triu(np.ones((tm, tm), np.float32), 1), BF16)
    return jnp.concatenate([rw_hi, rw_lo], axis=0), rw_hi, router_b.reshape(N_EXPERTS, 1).astype(F32), tri


def _post_call(kind, n, mixer_args, mixer_specs, x, lng, lnb, route_ops, scratch, seqs):
    tm = POST_TM
    rw2, rwhi, rb, tri = route_ops
    const = lambda shape: pl.BlockSpec(shape, lambda i: (0,) * len(shape))
    in_specs = list(mixer_specs) + [
        pl.BlockSpec((tm, D_MODEL), lambda i: (i, 0)),
        const((1, D_MODEL)), const((1, D_MODEL)),
        const((2 * N_EXPERTS, D_MODEL)), const((N_EXPERTS, D_MODEL)), const((N_EXPERTS, 1)), const((tm, tm))]
    out_shape = [jax.ShapeDtypeStruct((n, D_MODEL), F32),
                 jax.ShapeDtypeStruct((8, n), I32),
                 jax.ShapeDtypeStruct((8, n), F32),
                 jax.ShapeDtypeStruct((N_EXPERTS, 128), F32)]
    out_specs = [pl.BlockSpec((tm, D_MODEL), lambda i: (i, 0)),
                 pl.BlockSpec((8, tm), lambda i: (0, i)),
                 pl.BlockSpec((8, tm), lambda i: (0, i)),
                 const((N_EXPERTS, 128))]
    body = {"attn": _post_attn_kernel,
            "conv": functools.partial(_post_conv_kernel, seqs=seqs),
            "pool": functools.partial(_post_pool_kernel, seqs=seqs)}[kind]
    return pl.pallas_call(
        body,
        out_shape=out_shape,
        grid=(n // tm,),
        in_specs=in_specs,
        out_specs=out_specs,
        scratch_shapes=[pltpu.VMEM((N_EXPERTS, 128), F32)] + list(scratch),
        compiler_params=_cparams(("arbitrary",)),
        name="post_" + kind,
    )(*mixer_args, x, lng, lnb, rw2, rwhi, rb, tri)


def _moe_kernel(bexp, inv, x_hbm, wg_ref, wu_ref, wd_ref, out_hbm, xbuf, ybuf, gsem, ssem, *, n_tok):
    del bexp
    tm = ybuf.shape[1]
    b = pl.program_id(0)
    nb = pl.num_programs(0)
    slot = b % 2
    other = 1 - slot

    def gather_row(base, r, sl):
        tok = jnp.minimum(inv[base + r] >> 1, n_tok - 1)
        return pltpu.make_async_copy(x_hbm.at[pl.ds(tok, 1)], xbuf.at[sl, pl.ds(r, 1)], gsem.at[sl])

    def scatter_row(base, r, sl):
        code = inv[base + r]
        return pltpu.make_async_copy(ybuf.at[sl, pl.ds(r, 1)], out_hbm.at[code & 1, pl.ds(code >> 1, 1)],
                                     ssem.at[sl])

    def inv_base(blk):
        return (blk + MOE_LEAD_BLOCKS) * tm

    def start_rows(copy_row, blk, sl, unrolled):
        base = inv_base(blk)
        if unrolled:
            for r in range(tm):
                copy_row(base, r, sl).start()
        else:
            def body(r, carry):
                copy_row(base, r, sl).start()
                return carry
            lax.fori_loop(0, tm, body, 0)

    def gather_wait(sl):
        pltpu.make_async_copy(x_hbm.at[pl.ds(0, tm)], xbuf.at[sl], gsem.at[sl]).wait()

    def scatter_wait(sl):
        pltpu.make_async_copy(ybuf.at[sl], out_hbm.at[0, pl.ds(0, tm)], ssem.at[sl]).wait()

    @pl.when(b == 0)
    def _():
        ybuf[...] = jnp.zeros_like(ybuf)
        start_rows(gather_row, 0, 0, False)
        start_rows(scatter_row, -2, 0, False)

    gather_wait(slot)
    scatter_wait(slot)
    x = xbuf[slot].astype(BF16)
    start_rows(gather_row, jnp.minimum(b + 1, nb - 1), other, True)
    start_rows(scatter_row, b - 1, other, True)
    gate = jnp.dot(x, wg_ref[...], preferred_element_type=F32)
    up = jnp.dot(x, wu_ref[...], preferred_element_type=F32)
    hdn = (gate * jax.nn.sigmoid(gate) * up).astype(BF16)
    ybuf[slot] = jnp.dot(hdn, wd_ref[...], preferred_element_type=F32)

    @pl.when(b == nb - 1)
    def _():
        start_rows(scatter_row, b, slot, False)
        scatter_wait(slot)
        scatter_wait(other)
        gather_wait(other)


def _moe(x1, wg, wu, wd, bexp, inv, n_blocks):
    n = x1.shape[0]
    tm = MOE_TM
    wspec = lambda shape: pl.BlockSpec((None,) + shape, lambda b, bexp, inv: (bexp[b], 0, 0))
    grid_spec = pltpu.PrefetchScalarGridSpec(
        num_scalar_prefetch=2,
        grid=(n_blocks,),
        in_specs=[pl.BlockSpec(memory_space=pl.ANY),
                  wspec((D_MODEL, D_EXPERT)), wspec((D_MODEL, D_EXPERT)), wspec((D_EXPERT, D_MODEL))],
        out_specs=pl.BlockSpec(memory_space=pl.ANY),
        scratch_shapes=[pltpu.VMEM((2, tm, D_MODEL), F32), pltpu.VMEM((2, tm, D_MODEL), F32),
                        pltpu.SemaphoreType.DMA((2,)), pltpu.SemaphoreType.DMA((2,))],
    )
    return pl.pallas_call(
        functools.partial(_moe_kernel, n_tok=n),
        out_shape=jax.ShapeDtypeStruct((2, n + tm, D_MODEL), F32),
        grid_spec=grid_spec,
        compiler_params=_cparams(("arbitrary",)),
        name="moe_experts",
    )(bexp, inv, x1, wg, wu, wd)


def _dispatch_tables(ri, cnt, n):
    tm = MOE_TM
    n_blocks = -(-2 * n // tm) + N_EXPERTS
    experts = ri[0:2]
    rank = ri[2:4]
    counts = cnt[:, 0].astype(I32)
    nblk = (counts + tm - 1) // tm
    bend = jnp.cumsum(nblk)
    bstart = bend - nblk
    dest = (bstart[experts] + MOE_LEAD_BLOCKS) * tm + rank
    code = 2 * jnp.arange(n, dtype=I32)[None, :] + jnp.arange(2, dtype=I32)[:, None]
    rows = jnp.arange((n_blocks + MOE_LEAD_BLOCKS) * tm, dtype=I32)
    spare = 2 * (n + rows % tm) + (rows // tm) % 2
    inv = spare.at[dest.reshape(-1)].set(code.reshape(-1), unique_indices=True)
    blocks = jnp.arange(n_blocks, dtype=I32)
    bexp_raw = jnp.minimum(jnp.sum(blocks[:, None] >= bend[None, :], axis=1), N_EXPERTS - 1).astype(I32)
    bexp = bexp_raw[jnp.minimum(blocks, bend[-1] - 1)]
    return bexp, inv, n_blocks


def _ln2_kernel(x_ref, y0_ref, y1_ref, g_ref, lng_ref, lnb_ref, o_ref, *maybe_ob_ref):
    z = ALPHA * x_ref[...] + y0_ref[...] * g_ref[:, 0:1] + y1_ref[...] * g_ref[:, 1:2]
    mu = jnp.mean(z, axis=-1, keepdims=True)
    zc = z - mu
    var = jnp.mean(zc * zc, axis=-1, keepdims=True)
    out = zc * lax.rsqrt(var + LN_EPS) * lng_ref[...] + lnb_ref[...]
    o_ref[...] = out
    for ob_ref in maybe_ob_ref:
        ob_ref[...] = out.astype(BF16)


def _combine_ln(x1, y2, gates, lng, lnb, *, first_row=0, n_rows=None, with_bf16=True):
    tm = LN2_TM
    n_rows = x1.shape[0] if n_rows is None else n_rows
    assert first_row % tm == 0 and n_rows % tm == 0
    off = first_row // tm
    row = lambda i: (i + off, 0)
    const = lambda i: (0, 0)
    out_shape = [jax.ShapeDtypeStruct((n_rows, D_MODEL), F32)]
    out_specs = [pl.BlockSpec((tm, D_MODEL), lambda i: (i, 0))]
    if with_bf16:
        out_shape.append(jax.ShapeDtypeStruct((n_rows, D_MODEL), BF16))
        out_specs.append(pl.BlockSpec((tm, D_MODEL), lambda i: (i, 0)))
    return pl.pallas_call(
        _ln2_kernel,
        out_shape=out_shape,
        grid=(n_rows // tm,),
        in_specs=[pl.BlockSpec((tm, D_MODEL), row),
                  pl.BlockSpec((None, tm, D_MODEL), lambda i: (0, i + off, 0)),
                  pl.BlockSpec((None, tm, D_MODEL), lambda i: (1, i + off, 0)),
                  pl.BlockSpec((tm, 2), row),
                  pl.BlockSpec((1, D_MODEL), const), pl.BlockSpec((1, D_MODEL), const)],
        out_specs=out_specs,
        compiler_params=_cparams(("parallel",)),
        name="combine_ln",
    )(x1, y2, y2, gates, lng, lnb)


def _trunk(x, seqs, attn_w_qkv, attn_w_o, attn_rpb, pool_w, pool_scale, conv_w_in, conv_w, conv_w_out,
           router_w, router_b, expert_w_gate, expert_w_up, expert_w_down, ln_g, ln_b):
    n = x.shape[0]
    tm = POST_TM
    assert n % MM_TM == 0 and n % LN2_TM == 0 and all(l % tm == 0 for _, l in seqs)
    route_ops = _route_operands(router_w, router_b, tm)
    attn_tables = _attn_tables(seqs)
    xb = x.astype(BF16)
    row = lambda i: (i, 0)
    const2 = lambda i: (0, 0)
    for layer in range(DEPTH):
        kind, slot = layer % N_MIXERS, layer // N_MIXERS
        lng1, lnb1 = ln_g[layer, 0].reshape(1, D_MODEL), ln_b[layer, 0].reshape(1, D_MODEL)
        if kind == 0:
            qkv = _project(xb, attn_w_qkv[slot].astype(BF16), n_scaled=D_MODEL // MM_TN, scale=HEAD_DIM ** -0.5,
                           name="qkv_proj")
            o = _attention(qkv, _attn_bias(attn_rpb[slot]), attn_tables)
            args = (o, attn_w_o[slot].astype(BF16))
            specs = (pl.BlockSpec((tm, D_MODEL), row), pl.BlockSpec((D_MODEL, D_MODEL), const2))
            post = _post_call("attn", n, args, specs, x, lng1, lnb1, route_ops, (), seqs)
        elif kind == 1:
            nb8 = n // POOL_HALO
            args = (x, x, pool_w[slot].astype(BF16), pool_scale[slot].reshape(1, D_MODEL))
            specs = (pl.BlockSpec((POOL_HALO, D_MODEL), lambda i: (jnp.maximum(i * (tm // POOL_HALO) - 1, 0), 0)),
                     pl.BlockSpec((POOL_HALO, D_MODEL),
                                  lambda i: (jnp.minimum((i + 1) * (tm // POOL_HALO), nb8 - 1), 0)),
                     pl.BlockSpec((len(POOL_WINDOWS), POOL_GROUP_DIM, POOL_GROUP_DIM), lambda i: (0, 0, 0)),
                     pl.BlockSpec((1, D_MODEL), const2))
            scratch = (pltpu.VMEM((tm + 2 * POOL_HALO, D_MODEL), F32),)
            post = _post_call("pool", n, args, specs, x, lng1, lnb1, route_ops, scratch, seqs)
        else:
            bch = _project(xb, conv_w_in[slot].astype(BF16), name="conv_in_proj")
            nb16 = n // CONV_HALO
            prev = lambda c: pl.BlockSpec((CONV_HALO, D_MODEL),
                                          lambda i, c=c: (jnp.maximum(i * (tm // CONV_HALO) - 1, 0), c))
            nxt = lambda c: pl.BlockSpec((CONV_HALO, D_MODEL),
                                         lambda i, c=c: (jnp.minimum((i + 1) * (tm // CONV_HALO), nb16 - 1), c))
            args = (bch,) * 7 + (conv_w[slot], conv_w_out[slot].astype(BF16))
            specs = (pl.BlockSpec((tm, D_MODEL), lambda i: (i, 0)),
                     pl.BlockSpec((tm, D_MODEL), lambda i: (i, 1)),
                     pl.BlockSpec((tm, D_MODEL), lambda i: (i, 2)),
                     prev(1), prev(2), nxt(1), nxt(2),
                     pl.BlockSpec((3, D_MODEL), const2),
                     pl.BlockSpec((D_MODEL, D_MODEL), const2))
            scratch = (pltpu.VMEM((tm + 2 * POOL_HALO, D_MODEL), F32),)
            post = _post_call("conv", n, args, specs, x, lng1, lnb1, route_ops, scratch, seqs)
        x1, ri, rg, cnt = post
        bexp, inv, n_blocks = _dispatch_tables(ri, cnt, n)
        y2 = _moe(x1, expert_w_gate[layer].astype(BF16), expert_w_up[layer].astype(BF16),
                  expert_w_down[layer].astype(BF16), bexp, inv, n_blocks)
        gates = rg[0:2].T
        lng2, lnb2 = ln_g[layer, 1].reshape(1, D_MODEL), ln_b[layer, 1].reshape(1, D_MODEL)
        if layer + 1 < DEPTH:
            x, xb = _combine_ln(x1, y2, gates, lng2, lnb2)
    n0 = seqs[0][0] * seqs[0][1]
    (y_first,) = _combine_ln(x1, y2, gates, lng2, lnb2, first_row=0, n_rows=n0, with_bf16=False)
    (y_second,) = _combine_ln(x1, y2, gates, lng2, lnb2, first_row=n0, n_rows=n - n0, with_bf16=False)
    return y_first, y_second


def kernel(x_prompt, x_sample, attn_w_qkv, attn_w_o, attn_rpb, pool_w, pool_scale, conv_w_in, conv_w, conv_w_out,
           router_w, router_b, expert_w_gate, expert_w_up, expert_w_down, ln_g, ln_b):
    seqs = (x_prompt.shape[:2], x_sample.shape[:2])
    x = jnp.concatenate([x_prompt.reshape(-1, D_MODEL), x_sample.reshape(-1, D_MODEL)], axis=0)
    y_prompt, y_sample = _trunk(x, seqs, attn_w_qkv, attn_w_o, attn_rpb, pool_w, pool_scale, conv_w_in, conv_w,
                                conv_w_out, router_w, router_b, expert_w_gate, expert_w_up, expert_w_down, ln_g, ln_b)
    return y_prompt.reshape(x_prompt.shape), y_sample.reshape(x_sample.shape)
```

```python
import functools

import numpy as np
import jax
import jax.numpy as jnp
from jax import lax
from jax.experimental import pallas as pl
from jax.experimental.pallas import tpu as pltpu

F32 = jnp.float32
BF16 = jnp.bfloat16
I32 = jnp.int32

D_MODEL = 2048
DEPTH = 4
GRID_W = 64
N_MIXERS = 3
N_HEADS = 16
HEAD_DIM = D_MODEL // N_HEADS
WIN_ROWS = 8
WIN_COLS = 16
COL_BLOCK = 16
KEY_COLS = 2 * COL_BLOCK
N_COL_BLOCKS = GRID_W // COL_BLOCK
POOL_WINDOWS = (2, 4, 8, 16)
POOL_GROUP_DIM = D_MODEL // len(POOL_WINDOWS)
POOL_HALO = 8
CONV_HALO = 16
N_EXPERTS = 16
N_EXPERT_GROUPS = 4
EXPERTS_PER_GROUP = N_EXPERTS // N_EXPERT_GROUPS
D_EXPERT = D_MODEL // 2
ALPHA = (2 * DEPTH) ** 0.25
LN_EPS = 1e-5
MASK_VALUE = -1e30

MM_TM = 1024
MM_TN = 1024
ATT_ROWS = 8
ATT_TOK = ATT_ROWS * GRID_W
ATT_KROWS = 16
ATT_KSUB_ROWS = 4
ATT_KSUB_TOK = ATT_KSUB_ROWS * GRID_W
ATT_N_KSUB = ATT_KROWS // ATT_KSUB_ROWS
ATT_HEADS_PER_STEP = 2
POST_TM = 256
MOE_TM = 256
F32_SUBLANES = 8
MOE_LEAD_BLOCKS = 2
LN2_TM = 512
VMEM_LIMIT = 56 * 1024 * 1024


def _cparams(sem):
    return pltpu.CompilerParams(dimension_semantics=sem, vmem_limit_bytes=VMEM_LIMIT)


def _mm_kernel(x_ref, w_ref, o_ref, *, n_scaled, scale):
    acc = jnp.dot(x_ref[...], w_ref[...], preferred_element_type=F32)
    if n_scaled:
        acc = acc * jnp.where(pl.program_id(1) < n_scaled, F32(scale), F32(1.0))
    o_ref[...] = acc.astype(o_ref.dtype)


def _project(x, w, *, n_scaled=0, scale=1.0, name):
    m, k = x.shape
    n = w.shape[1]
    assert m % MM_TM == 0 and n % MM_TN == 0
    return pl.pallas_call(
        functools.partial(_mm_kernel, n_scaled=n_scaled, scale=scale),
        out_shape=jax.ShapeDtypeStruct((m, n), BF16),
        grid=(m // MM_TM, n // MM_TN),
        in_specs=[pl.BlockSpec((MM_TM, k), lambda i, j: (i, 0)),
                  pl.BlockSpec((k, MM_TN), lambda i, j: (0, j))],
        out_specs=pl.BlockSpec((MM_TM, MM_TN), lambda i, j: (i, j)),
        compiler_params=_cparams(("parallel", "arbitrary")),
        name=name,
    )(x, w)


def _key_col_start(j):
    return int(np.clip(j * COL_BLOCK - WIN_COLS // 2, 0, GRID_W - KEY_COLS))


def _attn_tables(seqs):
    blocks = []
    row0 = 0
    for batch, length in seqs:
        rows = length // GRID_W
        assert rows % ATT_ROWS == 0 and rows >= ATT_KROWS
        for _ in range(batch):
            for r0 in range(0, rows, ATT_ROWS):
                variant = 0 if r0 == 0 else (2 if r0 == rows - ATT_ROWS else 1)
                ks = row0 + int(np.clip(r0 - WIN_ROWS // 2, 0, rows - ATT_KROWS))
                blocks.append((variant, (row0 + r0) // ATT_ROWS, ks // ATT_KSUB_ROWS))
            row0 += rows
    blocks.sort(key=lambda b: b[0])
    var = np.array([b[0] for b in blocks], np.int32)
    gtab = np.array([b[1] for b in blocks], np.int32)
    kwin = np.array([b[2] for b in blocks], np.int32)
    return gtab, kwin, var


def _attn_bias(rpb):
    i = np.arange(ATT_ROWS)[:, None, None, None]
    cq = np.arange(COL_BLOCK)[None, :, None, None]
    rk = np.arange(ATT_KROWS)[None, None, :, None]
    ck = np.arange(KEY_COLS)[None, None, None, :]
    n_dy, n_dx = 2 * WIN_ROWS - 1, 2 * WIN_COLS - 1
    sel_dy = np.zeros((3, n_dy, ATT_ROWS, ATT_KROWS), np.float32)
    row_ok = np.zeros((3, ATT_ROWS, ATT_KROWS), bool)
    for variant in range(3):
        if variant == 0:
            rel_start = np.maximum(i - WIN_ROWS // 2, 0)
            dy = rk - i + WIN_ROWS - 1
        elif variant == 1:
            rel_start = i
            dy = rk - i + WIN_ROWS - 1 - WIN_ROWS // 2
        else:
            rel_start = np.minimum(i + WIN_ROWS // 2, ATT_KROWS - WIN_ROWS)
            dy = rk - i + WIN_ROWS - 1 - (ATT_KROWS - ATT_ROWS)
        row_ok[variant] = ((rk >= rel_start) & (rk < rel_start + WIN_ROWS))[:, 0, :, 0]
        dyc = np.clip(dy, 0, n_dy - 1)[:, 0, :, 0]
        sel_dy[variant] = np.arange(n_dy)[:, None, None] == dyc[None]
    sel_dx = np.zeros((N_COL_BLOCKS, n_dx, COL_BLOCK, KEY_COLS), np.float32)
    col_ok = np.zeros((N_COL_BLOCKS, COL_BLOCK, KEY_COLS), bool)
    for j in range(N_COL_BLOCKS):
        qc = j * COL_BLOCK + cq
        kc = _key_col_start(j) + ck
        c_start = np.clip(qc - WIN_COLS // 2, 0, GRID_W - WIN_COLS)
        col_ok[j] = ((kc >= c_start) & (kc < c_start + WIN_COLS))[0, :, 0, :]
        dxc = np.clip(kc - qc + WIN_COLS - 1, 0, n_dx - 1)[0, :, 0, :]
        sel_dx[j] = np.arange(n_dx)[:, None, None] == dxc[None]
    hp = lax.Precision.HIGHEST
    by = jnp.einsum("hab,vair->vhirb", rpb.astype(F32), jnp.asarray(sel_dy), precision=hp)
    bias = jnp.einsum("vhirb,jbck->vhjicrk", by, jnp.asarray(sel_dx), precision=hp)
    ok = row_ok[:, None, None, :, None, :, None] & col_ok[None, None, :, None, :, None, :]
    bias = jnp.where(jnp.asarray(ok), bias, F32(MASK_VALUE))
    return bias.reshape(3, N_HEADS, N_COL_BLOCKS, ATT_ROWS * COL_BLOCK, ATT_KROWS * KEY_COLS)


def _attn_kernel(gtab, kwin, var, q_ref, k0, k1, k2, k3, v0, v1, v2, v3, bias_ref, o_ref, kf_ref, vf_ref):
    del gtab, kwin, var
    krefs = (k0, k1, k2, k3)
    vrefs = (v0, v1, v2, v3)
    nt = (((1,), (1,)), ((), ()))
    lanes = [slice(hh * HEAD_DIM, (hh + 1) * HEAD_DIM) for hh in range(ATT_HEADS_PER_STEP)]
    work = [(hh, j) for hh in range(ATT_HEADS_PER_STEP) for j in range(N_COL_BLOCKS)]

    def window(ref, hh, j):
        kc0 = _key_col_start(j)
        return jnp.concatenate(
            [ref[hh, rk * GRID_W + kc0:rk * GRID_W + kc0 + KEY_COLS, :] for rk in range(ATT_KROWS)],
            axis=0).astype(BF16)

    for hh in range(ATT_HEADS_PER_STEP):
        for i in range(ATT_N_KSUB):
            kf_ref[hh, i * ATT_KSUB_TOK:(i + 1) * ATT_KSUB_TOK, :] = krefs[i][:, lanes[hh]].astype(F32)
            vf_ref[hh, i * ATT_KSUB_TOK:(i + 1) * ATT_KSUB_TOK, :] = vrefs[i][:, lanes[hh]].astype(F32)
    scores = []
    for hh, j in work:
        qj = jnp.concatenate(
            [q_ref[rr * GRID_W + j * COL_BLOCK:rr * GRID_W + (j + 1) * COL_BLOCK, lanes[hh]]
             for rr in range(ATT_ROWS)], axis=0)
        scores.append(lax.dot_general(qj, window(kf_ref, hh, j), nt, preferred_element_type=F32) + bias_ref[hh, j])
    probs = []
    for s in scores:
        m = jnp.max(s, axis=-1, keepdims=True)
        p = jnp.exp(s - m)
        probs.append((p.astype(BF16), jnp.sum(p, axis=-1, keepdims=True)))
    for (hh, j), (p, denom) in zip(work, probs):
        o = jnp.dot(p, window(vf_ref, hh, j), preferred_element_type=F32) / denom
        ob = o.astype(BF16)
        for rr in range(ATT_ROWS):
            o_ref[rr * GRID_W + j * COL_BLOCK:rr * GRID_W + (j + 1) * COL_BLOCK, lanes[hh]] = \
                ob[rr * COL_BLOCK:(rr + 1) * COL_BLOCK, :]


def _attention(qkv, bias, tables):
    n = qkv.shape[0]
    gtab, kwin, var = tables
    hw = ATT_HEADS_PER_STEP * HEAD_DIM
    n_hsteps = N_HEADS // ATT_HEADS_PER_STEP
    kv_specs = []
    for part in (1, 2):
        for i in range(ATT_N_KSUB):
            kv_specs.append(pl.BlockSpec(
                (ATT_KSUB_TOK, hw),
                lambda hp, s, gtab, kwin, var, i=i, part=part: (kwin[s] + i, part * n_hsteps + hp)))
    grid_spec = pltpu.PrefetchScalarGridSpec(
        num_scalar_prefetch=3,
        grid=(n_hsteps, len(gtab)),
        in_specs=[pl.BlockSpec((ATT_TOK, hw), lambda hp, s, gtab, kwin, var: (gtab[s], hp))] + kv_specs + [
            pl.BlockSpec((None, ATT_HEADS_PER_STEP, N_COL_BLOCKS, ATT_ROWS * COL_BLOCK, ATT_KROWS * KEY_COLS),
                         lambda hp, s, gtab, kwin, var: (var[s], hp, 0, 0, 0))],
        out_specs=pl.BlockSpec((ATT_TOK, hw), lambda hp, s, gtab, kwin, var: (gtab[s], hp)),
        scratch_shapes=[pltpu.VMEM((ATT_HEADS_PER_STEP, ATT_KROWS * GRID_W, HEAD_DIM), F32),
                        pltpu.VMEM((ATT_HEADS_PER_STEP, ATT_KROWS * GRID_W, HEAD_DIM), F32)],
    )
    return pl.pallas_call(
        _attn_kernel,
        out_shape=jax.ShapeDtypeStruct((n, D_MODEL), BF16),
        grid_spec=grid_spec,
        compiler_params=_cparams(("parallel", "arbitrary")),
        name="nbr_attention",
    )(jnp.asarray(gtab), jnp.asarray(kwin), jnp.asarray(var), qkv, *([qkv] * (2 * ATT_N_KSUB)), bias)


def _first_max4(vals):
    best = vals[0]
    idx = jnp.zeros(best.shape, I32)
    for i in range(1, 4):
        take = vals[i] > best
        best = jnp.where(take, vals[i], best)
        idx = jnp.where(take, I32(i), idx)
    return idx


def _select4(idx, vals):
    return jnp.where(idx == 0, vals[0], jnp.where(idx == 1, vals[1], jnp.where(idx == 2, vals[2], vals[3])))


def _ln_route(z, lng_ref, lnb_ref, rw2_ref, rwhi_ref, rb_ref, tri_ref, x1_ref, ri_ref, rg_ref, cnt_ref, carry_ref):
    tm = z.shape[0]

    @pl.when(pl.program_id(0) == 0)
    def _():
        carry_ref[...] = jnp.zeros_like(carry_ref)

    mu = jnp.mean(z, axis=-1, keepdims=True)
    zc = z - mu
    var = jnp.mean(zc * zc, axis=-1, keepdims=True)
    x1 = zc * lax.rsqrt(var + LN_EPS) * lng_ref[...] + lnb_ref[...]
    x1_ref[...] = x1

    hi = x1.astype(BF16)
    lo = (x1 - hi.astype(F32)).astype(BF16)
    nt = (((1,), (1,)), ((), ()))
    l2 = lax.dot_general(rw2_ref[...], hi, nt, preferred_element_type=F32)
    l1 = lax.dot_general(rwhi_ref[...], lo, nt, preferred_element_type=F32)
    logits = l2[:N_EXPERTS] + l2[N_EXPERTS:] + l1
    ex = jnp.exp(logits - jnp.max(logits, axis=0, keepdims=True))
    scores = ex / jnp.sum(ex, axis=0, keepdims=True)
    biased = scores + rb_ref[...]
    sr = [scores[e:e + 1, :] for e in range(N_EXPERTS)]
    br = [biased[e:e + 1, :] for e in range(N_EXPERTS)]

    group_scores = []
    for g in range(N_EXPERT_GROUPS):
        a, b, c, d = br[g * EXPERTS_PER_GROUP:(g + 1) * EXPERTS_PER_GROUP]
        hi1, lo1, hi2, lo2 = jnp.maximum(a, b), jnp.minimum(a, b), jnp.maximum(c, d), jnp.minimum(c, d)
        group_scores.append(jnp.maximum(hi1, hi2) + jnp.maximum(jnp.minimum(hi1, hi2), jnp.maximum(lo1, lo2)))
    gsel = _first_max4(group_scores)
    vb = [_select4(gsel, [br[g * EXPERTS_PER_GROUP + i] for g in range(N_EXPERT_GROUPS)]) for i in range(4)]
    vs = [_select4(gsel, [sr[g * EXPERTS_PER_GROUP + i] for g in range(N_EXPERT_GROUPS)]) for i in range(4)]
    i1 = _first_max4(vb)
    i2 = _first_max4([jnp.where(i1 == i, -jnp.inf, vb[i]) for i in range(4)])
    e0 = gsel * EXPERTS_PER_GROUP + i1
    e1 = gsel * EXPERTS_PER_GROUP + i2
    w0 = _select4(i1, vs)
    w1 = _select4(i2, vs)
    den = w0 + w1

    eio = lax.broadcasted_iota(I32, (N_EXPERTS, tm), 0)
    oh0 = eio == e0
    oh1 = eio == e1
    ohs = jnp.where(oh0 | oh1, F32(1.0), F32(0.0))
    carry = carry_ref[:, 0:1]
    before = jnp.dot(ohs.astype(BF16), tri_ref[...], preferred_element_type=F32) + carry
    r0 = jnp.sum(jnp.where(oh0, before, 0.0), axis=0, keepdims=True)
    r1 = jnp.sum(jnp.where(oh1, before, 0.0), axis=0, keepdims=True)
    carry_ref[...] = jnp.broadcast_to(carry + jnp.sum(ohs, axis=1, keepdims=True), carry_ref.shape)
    cnt_ref[...] = carry_ref[...]
    ri_ref[...] = jnp.concatenate([e0, e1, r0.astype(I32), r1.astype(I32), jnp.zeros((4, tm), I32)], axis=0)
    rg_ref[...] = jnp.concatenate([w0 / den, w1 / den, jnp.zeros((6, tm), F32)], axis=0)


def _seq_position(tm, seqs):
    (b0, l0), (_, l1) = seqs
    start = pl.program_id(0) * tm
    in_first = start < b0 * l0
    length = jnp.where(in_first, I32(l0), I32(l1))
    pos0 = jnp.where(in_first, start % l0, (start - b0 * l0) % l1)
    return pos0 + lax.broadcasted_iota(I32, (tm, 1), 0), length


def _post_attn_kernel(o_ref, w_ref, x_ref, *rest):
    y = jnp.dot(o_ref[...], w_ref[...], preferred_element_type=F32)
    _ln_route(ALPHA * x_ref[...] + y, *rest)


def _post_conv_kernel(b_ref, c_ref, h_ref, cp_ref, hp_ref, cn_ref, hn_ref, cw_ref, w_ref, x_ref, *rest, seqs):
    *route, u_ref = rest
    tm = b_ref.shape[0]
    pos, length = _seq_position(tm, seqs)
    h8 = POOL_HALO
    u_ref[0:h8, :] = (cp_ref[...].astype(F32) * hp_ref[...].astype(F32))[CONV_HALO - h8:, :]
    u_ref[h8:h8 + tm, :] = c_ref[...].astype(F32) * h_ref[...].astype(F32)
    u_ref[h8 + tm:, :] = (cn_ref[...].astype(F32) * hn_ref[...].astype(F32))[0:h8, :]
    prev = jnp.where(pos >= 1, u_ref[h8 - 1:h8 - 1 + tm, :], 0.0)
    nxt = jnp.where(pos + 1 < length, u_ref[h8 + 1:h8 + 1 + tm, :], 0.0)
    conv = prev * cw_ref[0:1, :] + u_ref[h8:h8 + tm, :] * cw_ref[1:2, :] + nxt * cw_ref[2:3, :]
    zin = (b_ref[...].astype(F32) * conv).astype(BF16)
    y = jnp.dot(zin, w_ref[...], preferred_element_type=F32)
    _ln_route(ALPHA * x_ref[...] + y, *route)


def _post_pool_kernel(xp_ref, xn_ref, w_ref, sc_ref, x_ref, *rest, seqs):
    *route, xs_ref = rest
    tm = x_ref.shape[0]
    pos, length = _seq_position(tm, seqs)
    h8 = POOL_HALO
    xs_ref[0:h8, :] = xp_ref[...]
    xs_ref[h8:h8 + tm, :] = x_ref[...]
    xs_ref[h8 + tm:, :] = xn_ref[...]
    ys = []
    for g, w in enumerate(POOL_WINDOWS):
        cs = slice(g * POOL_GROUP_DIM, (g + 1) * POOL_GROUP_DIM)
        acc = jnp.zeros((tm, POOL_GROUP_DIM), F32)
        for s in range(-(w // 2), w - w // 2):
            ok = (pos + s >= 0) & (pos + s < length)
            acc = acc + jnp.where(ok, xs_ref[h8 + s:h8 + s + tm, cs], 0.0)
        lo = jnp.maximum(pos - w // 2, 0)
        hi = jnp.minimum(pos - w // 2 + w, length)
        pooled = acc / (hi - lo).astype(F32) - x_ref[:, cs]
        ys.append(jnp.dot(pooled.astype(BF16), w_ref[g], preferred_element_type=F32))
    y = jnp.concatenate(ys, axis=1) * sc_ref[...]
    _ln_route(ALPHA * x_ref[...] + y, *route)


def _route_operands(router_w, router_b, tm):
    rw_t = router_w.T
    rw_hi = rw_t.astype(BF16)
    rw_lo = (rw_t - rw_hi.astype(F32)).astype(BF16)
    tri = jnp.asarray(np.triu(np.ones((tm, tm), np.float32), 1), BF16)
    return jnp.concatenate([rw_hi, rw_lo], axis=0), rw_hi, router_b.reshape(N_EXPERTS, 1).astype(F32), tri


def _post_call(kind, n, mixer_args, mixer_specs, x, lng, lnb, route_ops, scratch, seqs):
    tm = POST_TM
    rw2, rwhi, rb, tri = route_ops
    const = lambda shape: pl.BlockSpec(shape, lambda i: (0,) * len(shape))
    in_specs = list(mixer_specs) + [
        pl.BlockSpec((tm, D_MODEL), lambda i: (i, 0)),
        const((1, D_MODEL)), const((1, D_MODEL)),
        const((2 * N_EXPERTS, D_MODEL)), const((N_EXPERTS, D_MODEL)), const((N_EXPERTS, 1)), const((tm, tm))]
    out_shape = [jax.ShapeDtypeStruct((n, D_MODEL), F32),
                 jax.ShapeDtypeStruct((8, n), I32),
                 jax.ShapeDtypeStruct((8, n), F32),
                 jax.ShapeDtypeStruct((N_EXPERTS, 128), F32)]
    out_specs = [pl.BlockSpec((tm, D_MODEL), lambda i: (i, 0)),
                 pl.BlockSpec((8, tm), lambda i: (0, i)),
                 pl.BlockSpec((8, tm), lambda i: (0, i)),
                 const((N_EXPERTS, 128))]
    body = {"attn": _post_attn_kernel,
            "conv": functools.partial(_post_conv_kernel, seqs=seqs),
            "pool": functools.partial(_post_pool_kernel, seqs=seqs)}[kind]
    return pl.pallas_call(
        body,
        out_shape=out_shape,
        grid=(n // tm,),
        in_specs=in_specs,
        out_specs=out_specs,
        scratch_shapes=[pltpu.VMEM((N_EXPERTS, 128), F32)] + list(scratch),
        compiler_params=_cparams(("arbitrary",)),
        name="post_" + kind,
    )(*mixer_args, x, lng, lnb, rw2, rwhi, rb, tri)


def _moe_kernel(bexp, gsrc_ref, sdst_ref, x_hbm, wg_ref, wu_ref, wd_ref, out_hbm, xbuf, ybuf, gsem, ssem):
    del bexp
    groups, sub = ybuf.shape[1], ybuf.shape[2]
    tm = groups * sub
    s = pl.program_id(0)
    last = pl.num_programs(0) - 1
    cur = s % 2
    prev = 1 - cur

    def gather_row(g, u):
        tok = gsrc_ref[0, g * sub + u]
        return pltpu.make_async_copy(x_hbm.at[pl.ds(tok, 1)], xbuf.at[cur, g, pl.ds(u, 1)], gsem.at[cur])

    def scatter_row(g, u):
        row = sdst_ref[0, g * sub + u]
        return pltpu.make_async_copy(ybuf.at[cur, g, pl.ds(u, 1)], out_hbm.at[pl.ds(row, 1)], ssem.at[cur])

    def gather_wait(sl):
        pltpu.make_async_copy(xbuf.at[sl], xbuf.at[sl], gsem.at[sl]).wait()

    def scatter_wait(sl):
        pltpu.make_async_copy(ybuf.at[sl], ybuf.at[sl], ssem.at[sl]).wait()

    @pl.when(s == 0)
    def _():
        ybuf[...] = jnp.zeros_like(ybuf)

    @pl.when(s > 0)
    def _():
        gather_wait(prev)
        scatter_wait(prev)

    def start_rows(g, carry):
        for u in range(sub):
            gather_row(g, u).start()
            scatter_row(g, u).start()
        return carry
    lax.fori_loop(0, groups, start_rows, 0)

    @pl.when((s > 0) & (s < last))
    def _():
        x = xbuf[prev].reshape(tm, D_MODEL).astype(BF16)
        gate = jnp.dot(x, wg_ref[...], preferred_element_type=F32)
        up = jnp.dot(x, wu_ref[...], preferred_element_type=F32)
        hdn = (gate * jax.nn.sigmoid(gate) * up).astype(BF16)
        ybuf[prev] = jnp.dot(hdn, wd_ref[...], preferred_element_type=F32).reshape(groups, sub, D_MODEL)

    @pl.when(s == last)
    def _():
        gather_wait(cur)
        scatter_wait(cur)


def _moe(x1, wg, wu, wd, layer, bexp, gsrc, sdst, n_blocks):
    n = x1.shape[0]
    tm = MOE_TM
    block_of_step = lambda s: jnp.clip(s - 1, 0, n_blocks - 1)
    wspec = lambda shape: pl.BlockSpec((None, None) + shape, lambda s, bexp: (layer, bexp[block_of_step(s)], 0, 0))
    smem_row = lambda lead: pl.BlockSpec((None, 1, tm), lambda s, bexp: (s + lead, 0, 0), memory_space=pltpu.SMEM)
    grid_spec = pltpu.PrefetchScalarGridSpec(
        num_scalar_prefetch=1,
        grid=(n_blocks + 2,),
        in_specs=[smem_row(MOE_LEAD_BLOCKS), smem_row(MOE_LEAD_BLOCKS - 2),
                  pl.BlockSpec(memory_space=pl.ANY),
                  wspec((D_MODEL, D_EXPERT)), wspec((D_MODEL, D_EXPERT)), wspec((D_EXPERT, D_MODEL))],
        out_specs=pl.BlockSpec(memory_space=pl.ANY),
        scratch_shapes=[pltpu.VMEM((2, tm // F32_SUBLANES, F32_SUBLANES, D_MODEL), F32),
                        pltpu.VMEM((2, tm // F32_SUBLANES, F32_SUBLANES, D_MODEL), F32),
                        pltpu.SemaphoreType.DMA((2,)), pltpu.SemaphoreType.DMA((2,))],
    )
    return pl.pallas_call(
        _moe_kernel,
        out_shape=jax.ShapeDtypeStruct((2 * n + tm, D_MODEL), F32),
        grid_spec=grid_spec,
        compiler_params=_cparams(("arbitrary",)),
        name="moe_experts",
    )(bexp, gsrc, sdst, x1, wg, wu, wd)


def _dispatch_tables(ri, cnt, n):
    tm = MOE_TM
    n_blocks = -(-2 * n // tm) + N_EXPERTS
    n_table_blocks = n_blocks + 2 * MOE_LEAD_BLOCKS
    experts = ri[0:2]
    rank = ri[2:4]
    counts = cnt[:, 0].astype(I32)
    nblk = (counts + tm - 1) // tm
    bend = jnp.cumsum(nblk)
    bstart = bend - nblk
    first_block = jnp.sum(jnp.where(experts[:, :, None] == jnp.arange(N_EXPERTS, dtype=I32), bstart, 0), axis=-1)
    dest = (first_block + MOE_LEAD_BLOCKS) * tm + rank
    code = 2 * jnp.arange(n, dtype=I32)[None, :] + jnp.arange(2, dtype=I32)[:, None]
    rows = jnp.arange(n_table_blocks * tm, dtype=I32)
    inv = jnp.full(rows.shape, -1, I32).at[dest.reshape(-1)].set(code.reshape(-1), unique_indices=True)
    gsrc = jnp.where(inv < 0, n - 1, inv >> 1)
    sdst = jnp.where(inv < 0, 2 * n + rows % tm, (inv & 1) * n + (inv >> 1))
    blocks = jnp.arange(n_blocks, dtype=I32)
    bexp_raw = jnp.minimum(jnp.sum(blocks[:, None] >= bend[None, :], axis=1), N_EXPERTS - 1).astype(I32)
    bexp = bexp_raw[jnp.minimum(blocks, bend[-1] - 1)]
    shape3 = (n_table_blocks, 1, tm)
    return bexp, gsrc.reshape(shape3), sdst.reshape(shape3), n_blocks


def _ln2_kernel(x_ref, y0_ref, y1_ref, g_ref, lng_ref, lnb_ref, o_ref, *maybe_ob_ref):
    z = ALPHA * x_ref[...] + y0_ref[...] * g_ref[:, 0:1] + y1_ref[...] * g_ref[:, 1:2]
    mu = jnp.mean(z, axis=-1, keepdims=True)
    zc = z - mu
    var = jnp.mean(zc * zc, axis=-1, keepdims=True)
    out = zc * lax.rsqrt(var + LN_EPS) * lng_ref[...] + lnb_ref[...]
    o_ref[...] = out
    for ob_ref in maybe_ob_ref:
        ob_ref[...] = out.astype(BF16)


def _combine_ln(x1, y2, gates, lng, lnb, *, first_row=0, n_rows=None, with_bf16=True):
    tm = LN2_TM
    n_rows = x1.shape[0] if n_rows is None else n_rows
    assert first_row % tm == 0 and n_rows % tm == 0
    off = first_row // tm
    slot1 = x1.shape[0] // tm
    row = lambda i: (i + off, 0)
    const = lambda i: (0, 0)
    out_shape = [jax.ShapeDtypeStruct((n_rows, D_MODEL), F32)]
    out_specs = [pl.BlockSpec((tm, D_MODEL), lambda i: (i, 0))]
    if with_bf16:
        out_shape.append(jax.ShapeDtypeStruct((n_rows, D_MODEL), BF16))
        out_specs.append(pl.BlockSpec((tm, D_MODEL), lambda i: (i, 0)))
    return pl.pallas_call(
        _ln2_kernel,
        out_shape=out_shape,
        grid=(n_rows // tm,),
        in_specs=[pl.BlockSpec((tm, D_MODEL), row),
                  pl.BlockSpec((tm, D_MODEL), row),
                  pl.BlockSpec((tm, D_MODEL), lambda i: (i + off + slot1, 0)),
                  pl.BlockSpec((tm, 2), row),
                  pl.BlockSpec((1, D_MODEL), const), pl.BlockSpec((1, D_MODEL), const)],
        out_specs=out_specs,
        compiler_params=_cparams(("parallel",)),
        name="combine_ln",
    )(x1, y2, y2, gates, lng, lnb)


def _trunk(x, seqs, attn_w_qkv, attn_w_o, attn_rpb, pool_w, pool_scale, conv_w_in, conv_w, conv_w_out,
           router_w, router_b, expert_w_gate, expert_w_up, expert_w_down, ln_g, ln_b):
    n = x.shape[0]
    tm = POST_TM
    assert n % MM_TM == 0 and n % LN2_TM == 0 and all(l % tm == 0 for _, l in seqs)
    route_ops = _route_operands(router_w, router_b, tm)
    attn_tables = _attn_tables(seqs)
    xb = x.astype(BF16)
    wg_all, wu_all, wd_all = (w.astype(BF16) for w in (expert_w_gate, expert_w_up, expert_w_down))
    row = lambda i: (i, 0)
    const2 = lambda i: (0, 0)
    for layer in range(DEPTH):
        kind, slot = layer % N_MIXERS, layer // N_MIXERS
        lng1, lnb1 = ln_g[layer, 0].reshape(1, D_MODEL), ln_b[layer, 0].reshape(1, D_MODEL)
        if kind == 0:
            qkv = _project(xb, attn_w_qkv[slot].astype(BF16), n_scaled=D_MODEL // MM_TN, scale=HEAD_DIM ** -0.5,
                           name="qkv_proj")
            o = _attention(qkv, _attn_bias(attn_rpb[slot]), attn_tables)
            args = (o, attn_w_o[slot].astype(BF16))
            specs = (pl.BlockSpec((tm, D_MODEL), row), pl.BlockSpec((D_MODEL, D_MODEL), const2))
            post = _post_call("attn", n, args, specs, x, lng1, lnb1, route_ops, (), seqs)
        elif kind == 1:
            nb8 = n // POOL_HALO
            args = (x, x, pool_w[slot].astype(BF16), pool_scale[slot].reshape(1, D_MODEL))
            specs = (pl.BlockSpec((POOL_HALO, D_MODEL), lambda i: (jnp.maximum(i * (tm // POOL_HALO) - 1, 0), 0)),
                     pl.BlockSpec((POOL_HALO, D_MODEL),
                                  lambda i: (jnp.minimum((i + 1) * (tm // POOL_HALO), nb8 - 1), 0)),
                     pl.BlockSpec((len(POOL_WINDOWS), POOL_GROUP_DIM, POOL_GROUP_DIM), lambda i: (0, 0, 0)),
                     pl.BlockSpec((1, D_MODEL), const2))
            scratch = (pltpu.VMEM((tm + 2 * POOL_HALO, D_MODEL), F32),)
            post = _post_call("pool", n, args, specs, x, lng1, lnb1, route_ops, scratch, seqs)
        else:
            bch = _project(xb, conv_w_in[slot].astype(BF16), name="conv_in_proj")
            nb16 = n // CONV_HALO
            prev = lambda c: pl.BlockSpec((CONV_HALO, D_MODEL),
                                          lambda i, c=c: (jnp.maximum(i * (tm // CONV_HALO) - 1, 0), c))
            nxt = lambda c: pl.BlockSpec((CONV_HALO, D_MODEL),
                                         lambda i, c=c: (jnp.minimum((i + 1) * (tm // CONV_HALO), nb16 - 1), c))
            args = (bch,) * 7 + (conv_w[slot], conv_w_out[slot].astype(BF16))
            specs = (pl.BlockSpec((tm, D_MODEL), lambda i: (i, 0)),
                     pl.BlockSpec((tm, D_MODEL), lambda i: (i, 1)),
                     pl.BlockSpec((tm, D_MODEL), lambda i: (i, 2)),
                     prev(1), prev(2), nxt(1), nxt(2),
                     pl.BlockSpec((3, D_MODEL), const2),
                     pl.BlockSpec((D_MODEL, D_MODEL), const2))
            scratch = (pltpu.VMEM((tm + 2 * POOL_HALO, D_MODEL), F32),)
            post = _post_call("conv", n, args, specs, x, lng1, lnb1, route_ops, scratch, seqs)
        x1, ri, rg, cnt = post
        bexp, gsrc, sdst, n_blocks = _dispatch_tables(ri, cnt, n)
        y2 = _moe(x1, wg_all, wu_all, wd_all, layer, bexp, gsrc, sdst, n_blocks)
        gates = rg[0:2].T
        lng2, lnb2 = ln_g[layer, 1].reshape(1, D_MODEL), ln_b[layer, 1].reshape(1, D_MODEL)
        if layer + 1 < DEPTH:
            x, xb = _combine_ln(x1, y2, gates, lng2, lnb2)
    n0 = seqs[0][0] * seqs[0][1]
    (y_first,) = _combine_ln(x1, y2, gates, lng2, lnb2, first_row=0, n_rows=n0, with_bf16=False)
    (y_second,) = _combine_ln(x1, y2, gates, lng2, lnb2, first_row=n0, n_rows=n - n0, with_bf16=False)
    return y_first, y_second


def kernel(x_prompt, x_sample, attn_w_qkv, attn_w_o, attn_rpb, pool_w, pool_scale, conv_w_in, conv_w, conv_w_out,
           router_w, router_b, expert_w_gate, expert_w_up, expert_w_down, ln_g, ln_b):
    seqs = (x_prompt.shape[:2], x_sample.shape[:2])
    x = jnp.concatenate([x_prompt.reshape(-1, D_MODEL), x_sample.reshape(-1, D_MODEL)], axis=0)
    y_prompt, y_sample = _trunk(x, seqs, attn_w_qkv, attn_w_o, attn_rpb, pool_w, pool_scale, conv_w_in, conv_w,
                                conv_w_out, router_w, router_b, expert_w_gate, expert_w_up, expert_w_down, ln_g, ln_b)
    return y_prompt.reshape(x_prompt.shape), y_sample.reshape(x_sample.shape)
```

```python
import functools

import numpy as np
import jax
import jax.numpy as jnp
from jax import lax
from jax.experimental import pallas as pl
from jax.experimental.pallas import tpu as pltpu

F32 = jnp.float32
BF16 = jnp.bfloat16
I32 = jnp.int32

D_MODEL = 2048
DEPTH = 4
GRID_W = 64
N_MIXERS = 3
N_HEADS = 16
HEAD_DIM = D_MODEL // N_HEADS
WIN_ROWS = 8
WIN_COLS = 16
COL_BLOCK = 16
KEY_COLS = 2 * COL_BLOCK
N_COL_BLOCKS = GRID_W // COL_BLOCK
POOL_WINDOWS = (2, 4, 8, 16)
POOL_GROUP_DIM = D_MODEL // len(POOL_WINDOWS)
POOL_HALO = 8
CONV_HALO = 16
N_EXPERTS = 16
N_EXPERT_GROUPS = 4
EXPERTS_PER_GROUP = N_EXPERTS // N_EXPERT_GROUPS
D_EXPERT = D_MODEL // 2
ALPHA = (2 * DEPTH) ** 0.25
LN_EPS = 1e-5
MASK_VALUE = -1e30

MM_TM = 1024
MM_TN = 1024
ATT_ROWS = 8
ATT_TOK = ATT_ROWS * GRID_W
ATT_KROWS = 16
ATT_KSUB_ROWS = 4
ATT_KSUB_TOK = ATT_KSUB_ROWS * GRID_W
ATT_N_KSUB = ATT_KROWS // ATT_KSUB_ROWS
ATT_HEADS_PER_STEP = 2
POST_TM = 256
MOE_TM = 256
F32_SUBLANES = 8
MOE_LEAD_BLOCKS = 2
LN2_TM = 512
VMEM_LIMIT = 56 * 1024 * 1024


def _cparams(sem):
    return pltpu.CompilerParams(dimension_semantics=sem, vmem_limit_bytes=VMEM_LIMIT)


def _mm_kernel(x_ref, w_ref, o_ref, *, n_scaled, scale):
    acc = jnp.dot(x_ref[...], w_ref[...], preferred_element_type=F32)
    if n_scaled:
        acc = acc * jnp.where(pl.program_id(1) < n_scaled, F32(scale), F32(1.0))
    o_ref[...] = acc.astype(o_ref.dtype)


def _project(x, w, *, n_scaled=0, scale=1.0, name):
    m, k = x.shape
    n = w.shape[1]
    assert m % MM_TM == 0 and n % MM_TN == 0
    return pl.pallas_call(
        functools.partial(_mm_kernel, n_scaled=n_scaled, scale=scale),
        out_shape=jax.ShapeDtypeStruct((m, n), BF16),
        grid=(m // MM_TM, n // MM_TN),
        in_specs=[pl.BlockSpec((MM_TM, k), lambda i, j: (i, 0)),
                  pl.BlockSpec((k, MM_TN), lambda i, j: (0, j))],
        out_specs=pl.BlockSpec((MM_TM, MM_TN), lambda i, j: (i, j)),
        compiler_params=_cparams(("parallel", "arbitrary")),
        name=name,
    )(x, w)


def _key_col_start(j):
    return int(np.clip(j * COL_BLOCK - WIN_COLS // 2, 0, GRID_W - KEY_COLS))


def _attn_tables(seqs):
    blocks = []
    row0 = 0
    for batch, length in seqs:
        rows = length // GRID_W
        assert rows % ATT_ROWS == 0 and rows >= ATT_KROWS
        for _ in range(batch):
            for r0 in range(0, rows, ATT_ROWS):
                variant = 0 if r0 == 0 else (2 if r0 == rows - ATT_ROWS else 1)
                ks = row0 + int(np.clip(r0 - WIN_ROWS // 2, 0, rows - ATT_KROWS))
                blocks.append((variant, (row0 + r0) // ATT_ROWS, ks // ATT_KSUB_ROWS))
            row0 += rows
    blocks.sort(key=lambda b: b[0])
    var = np.array([b[0] for b in blocks], np.int32)
    gtab = np.array([b[1] for b in blocks], np.int32)
    kwin = np.array([b[2] for b in blocks], np.int32)
    return gtab, kwin, var


def _attn_bias(rpb):
    i = np.arange(ATT_ROWS)[:, None, None, None]
    cq = np.arange(COL_BLOCK)[None, :, None, None]
    rk = np.arange(ATT_KROWS)[None, None, :, None]
    ck = np.arange(KEY_COLS)[None, None, None, :]
    n_dy, n_dx = 2 * WIN_ROWS - 1, 2 * WIN_COLS - 1
    sel_dy = np.zeros((3, n_dy, ATT_ROWS, ATT_KROWS), np.float32)
    row_ok = np.zeros((3, ATT_ROWS, ATT_KROWS), bool)
    for variant in range(3):
        if variant == 0:
            rel_start = np.maximum(i - WIN_ROWS // 2, 0)
            dy = rk - i + WIN_ROWS - 1
        elif variant == 1:
            rel_start = i
            dy = rk - i + WIN_ROWS - 1 - WIN_ROWS // 2
        else:
            rel_start = np.minimum(i + WIN_ROWS // 2, ATT_KROWS - WIN_ROWS)
            dy = rk - i + WIN_ROWS - 1 - (ATT_KROWS - ATT_ROWS)
        row_ok[variant] = ((rk >= rel_start) & (rk < rel_start + WIN_ROWS))[:, 0, :, 0]
        dyc = np.clip(dy, 0, n_dy - 1)[:, 0, :, 0]
        sel_dy[variant] = np.arange(n_dy)[:, None, None] == dyc[None]
    by = jnp.einsum("hab,vair->vhirb", rpb.astype(F32), jnp.asarray(sel_dy), precision=lax.Precision.HIGHEST)
    col_ok = np.zeros((N_COL_BLOCKS, COL_BLOCK, KEY_COLS), bool)
    per_block = []
    for j in range(N_COL_BLOCKS):
        qc = j * COL_BLOCK + cq
        kc = _key_col_start(j) + ck
        c_start = np.clip(qc - WIN_COLS // 2, 0, GRID_W - WIN_COLS)
        col_ok[j] = ((kc >= c_start) & (kc < c_start + WIN_COLS))[0, :, 0, :]
        per_query = []
        for c in range(COL_BLOCK):
            first = _key_col_start(j) - (j * COL_BLOCK + c) + WIN_COLS - 1
            lo, hi = max(first, 0), min(first + KEY_COLS, n_dx)
            assert not col_ok[j, c, :lo - first].any() and not col_ok[j, c, hi - first:].any()
            pad = [(0, 0)] * 4 + [(lo - first, first + KEY_COLS - hi)]
            per_query.append(jnp.pad(by[..., lo:hi], pad))
        per_block.append(jnp.stack(per_query, axis=3))
    bias = jnp.stack(per_block, axis=2)
    ok = row_ok[:, None, None, :, None, :, None] & col_ok[None, None, :, None, :, None, :]
    bias = jnp.where(jnp.asarray(ok), bias, F32(MASK_VALUE))
    return bias.reshape(3, N_HEADS, N_COL_BLOCKS, ATT_ROWS * COL_BLOCK, ATT_KROWS * KEY_COLS)


def _attn_kernel(gtab, kwin, var, q_ref, k0, k1, k2, k3, v0, v1, v2, v3, bias_ref, o_ref, kf_ref, vf_ref):
    del gtab, kwin, var
    krefs = (k0, k1, k2, k3)
    vrefs = (v0, v1, v2, v3)
    nt = (((1,), (1,)), ((), ()))
    lanes = [slice(hh * HEAD_DIM, (hh + 1) * HEAD_DIM) for hh in range(ATT_HEADS_PER_STEP)]
    work = [(hh, j) for hh in range(ATT_HEADS_PER_STEP) for j in range(N_COL_BLOCKS)]

    def window(ref, hh, j):
        kc0 = _key_col_start(j)
        return jnp.concatenate(
            [ref[hh, rk * GRID_W + kc0:rk * GRID_W + kc0 + KEY_COLS, :] for rk in range(ATT_KROWS)],
            axis=0).astype(BF16)

    for hh in range(ATT_HEADS_PER_STEP):
        for i in range(ATT_N_KSUB):
            kf_ref[hh, i * ATT_KSUB_TOK:(i + 1) * ATT_KSUB_TOK, :] = krefs[i][:, lanes[hh]].astype(F32)
            vf_ref[hh, i * ATT_KSUB_TOK:(i + 1) * ATT_KSUB_TOK, :] = vrefs[i][:, lanes[hh]].astype(F32)
    scores = []
    for hh, j in work:
        qj = jnp.concatenate(
            [q_ref[rr * GRID_W + j * COL_BLOCK:rr * GRID_W + (j + 1) * COL_BLOCK, lanes[hh]]
             for rr in range(ATT_ROWS)], axis=0)
        scores.append(lax.dot_general(qj, window(kf_ref, hh, j), nt, preferred_element_type=F32) + bias_ref[hh, j])
    probs = []
    for s in scores:
        m = jnp.max(s, axis=-1, keepdims=True)
        p = jnp.exp(s - m)
        probs.append((p.astype(BF16), jnp.sum(p, axis=-1, keepdims=True)))
    for (hh, j), (p, denom) in zip(work, probs):
        o = jnp.dot(p, window(vf_ref, hh, j), preferred_element_type=F32) / denom
        ob = o.astype(BF16)
        for rr in range(ATT_ROWS):
            o_ref[rr * GRID_W + j * COL_BLOCK:rr * GRID_W + (j + 1) * COL_BLOCK, lanes[hh]] = \
                ob[rr * COL_BLOCK:(rr + 1) * COL_BLOCK, :]


def _attention(qkv, bias, tables):
    n = qkv.shape[0]
    gtab, kwin, var = tables
    hw = ATT_HEADS_PER_STEP * HEAD_DIM
    n_hsteps = N_HEADS // ATT_HEADS_PER_STEP
    kv_specs = []
    for part in (1, 2):
        for i in range(ATT_N_KSUB):
            kv_specs.append(pl.BlockSpec(
                (ATT_KSUB_TOK, hw),
                lambda hp, s, gtab, kwin, var, i=i, part=part: (kwin[s] + i, part * n_hsteps + hp)))
    grid_spec = pltpu.PrefetchScalarGridSpec(
        num_scalar_prefetch=3,
        grid=(n_hsteps, len(gtab)),
        in_specs=[pl.BlockSpec((ATT_TOK, hw), lambda hp, s, gtab, kwin, var: (gtab[s], hp))] + kv_specs + [
            pl.BlockSpec((None, ATT_HEADS_PER_STEP, N_COL_BLOCKS, ATT_ROWS * COL_BLOCK, ATT_KROWS * KEY_COLS),
                         lambda hp, s, gtab, kwin, var: (var[s], hp, 0, 0, 0))],
        out_specs=pl.BlockSpec((ATT_TOK, hw), lambda hp, s, gtab, kwin, var: (gtab[s], hp)),
        scratch_shapes=[pltpu.VMEM((ATT_HEADS_PER_STEP, ATT_KROWS * GRID_W, HEAD_DIM), F32),
                        pltpu.VMEM((ATT_HEADS_PER_STEP, ATT_KROWS * GRID_W, HEAD_DIM), F32)],
    )
    return pl.pallas_call(
        _attn_kernel,
        out_shape=jax.ShapeDtypeStruct((n, D_MODEL), BF16),
        grid_spec=grid_spec,
        compiler_params=_cparams(("parallel", "arbitrary")),
        name="nbr_attention",
    )(jnp.asarray(gtab), jnp.asarray(kwin), jnp.asarray(var), qkv, *([qkv] * (2 * ATT_N_KSUB)), bias)


def _first_max4(vals):
    best = vals[0]
    idx = jnp.zeros(best.shape, I32)
    for i in range(1, 4):
        take = vals[i] > best
        best = jnp.where(take, vals[i], best)
        idx = jnp.where(take, I32(i), idx)
    return idx


def _select4(idx, vals):
    return jnp.where(idx == 0, vals[0], jnp.where(idx == 1, vals[1], jnp.where(idx == 2, vals[2], vals[3])))


def _ln_route(z, counted, lng_ref, lnb_ref, rw2_ref, rwhi_ref, rb_ref, tri_ref, x1_ref, ri_ref, rg_ref, cnt_ref,
              carry_ref):
    tm = z.shape[0]

    mu = jnp.mean(z, axis=-1, keepdims=True)
    zc = z - mu
    var = jnp.mean(zc * zc, axis=-1, keepdims=True)
    x1 = zc * lax.rsqrt(var + LN_EPS) * lng_ref[...] + lnb_ref[...]
    x1_ref[...] = x1

    hi = x1.astype(BF16)
    lo = (x1 - hi.astype(F32)).astype(BF16)
    nt = (((1,), (1,)), ((), ()))
    l2 = lax.dot_general(rw2_ref[...], hi, nt, preferred_element_type=F32)
    l1 = lax.dot_general(rwhi_ref[...], lo, nt, preferred_element_type=F32)
    logits = l2[:N_EXPERTS] + l2[N_EXPERTS:] + l1
    ex = jnp.exp(logits - jnp.max(logits, axis=0, keepdims=True))
    scores = ex / jnp.sum(ex, axis=0, keepdims=True)
    biased = scores + rb_ref[...]
    sr = [scores[e:e + 1, :] for e in range(N_EXPERTS)]
    br = [biased[e:e + 1, :] for e in range(N_EXPERTS)]

    group_scores = []
    for g in range(N_EXPERT_GROUPS):
        a, b, c, d = br[g * EXPERTS_PER_GROUP:(g + 1) * EXPERTS_PER_GROUP]
        hi1, lo1, hi2, lo2 = jnp.maximum(a, b), jnp.minimum(a, b), jnp.maximum(c, d), jnp.minimum(c, d)
        group_scores.append(jnp.maximum(hi1, hi2) + jnp.maximum(jnp.minimum(hi1, hi2), jnp.maximum(lo1, lo2)))
    gsel = _first_max4(group_scores)
    vb = [_select4(gsel, [br[g * EXPERTS_PER_GROUP + i] for g in range(N_EXPERT_GROUPS)]) for i in range(4)]
    vs = [_select4(gsel, [sr[g * EXPERTS_PER_GROUP + i] for g in range(N_EXPERT_GROUPS)]) for i in range(4)]
    i1 = _first_max4(vb)
    i2 = _first_max4([jnp.where(i1 == i, -jnp.inf, vb[i]) for i in range(4)])
    e0 = gsel * EXPERTS_PER_GROUP + i1
    e1 = gsel * EXPERTS_PER_GROUP + i2
    w0 = _select4(i1, vs)
    w1 = _select4(i2, vs)
    den = w0 + w1

    eio = lax.broadcasted_iota(I32, (N_EXPERTS, tm), 0)
    oh0 = eio == e0
    oh1 = eio == e1
    ohs = jnp.where(oh0 | oh1, F32(1.0), F32(0.0))
    carry = carry_ref[:, 0:1]
    before = jnp.dot(ohs.astype(BF16), tri_ref[...], preferred_element_type=F32) + carry
    r0 = jnp.sum(jnp.where(oh0, before, 0.0), axis=0, keepdims=True)
    r1 = jnp.sum(jnp.where(oh1, before, 0.0), axis=0, keepdims=True)
    tile_counts = jnp.where(counted, jnp.sum(ohs, axis=1, keepdims=True), 0.0)
    carry_ref[...] = jnp.broadcast_to(carry + tile_counts, carry_ref.shape)
    cnt_ref[...] = carry_ref[...]
    ri_ref[...] = jnp.concatenate([e0, e1, r0.astype(I32), r1.astype(I32), jnp.zeros((4, tm), I32)], axis=0)
    rg_ref[...] = jnp.concatenate([w0 / den, w1 / den, jnp.zeros((6, tm), F32)], axis=0)


def _seq_position(tile, tm, seqs):
    (b0, l0), (_, l1) = seqs
    start = tile * tm
    in_first = start < b0 * l0
    length = jnp.where(in_first, I32(l0), I32(l1))
    pos0 = jnp.where(in_first, start % l0, (start - b0 * l0) % l1)
    return pos0 + lax.broadcasted_iota(I32, (tm, 1), 0), length


def _mixer_tile():
    return jnp.minimum(pl.program_id(0), pl.num_programs(0) - 2)


def _pipelined_post(mix_fn, x_ref, route, y_ref, carry_ref):
    i = pl.program_id(0)

    @pl.when(i == 0)
    def _():
        y_ref[...] = jnp.zeros_like(y_ref)
        carry_ref[...] = jnp.zeros_like(carry_ref)

    z = ALPHA * x_ref[...] + y_ref[...]
    y_ref[...] = mix_fn()
    _ln_route(z, i > 0, *route, carry_ref)


def _post_attn_kernel(o_ref, w_ref, x_ref, *rest):
    *route, carry_ref, y_ref = rest
    _pipelined_post(lambda: jnp.dot(o_ref[...], w_ref[...], preferred_element_type=F32), x_ref, route, y_ref,
                    carry_ref)


def _post_conv_kernel(b_ref, c_ref, h_ref, cp_ref, hp_ref, cn_ref, hn_ref, cw_ref, w_ref, x_ref, *rest, seqs):
    *route, carry_ref, y_ref, u_ref = rest
    _pipelined_post(functools.partial(_conv_mix, b_ref, c_ref, h_ref, cp_ref, hp_ref, cn_ref, hn_ref, cw_ref, w_ref,
                                      u_ref, seqs), x_ref, route, y_ref, carry_ref)


def _conv_mix(b_ref, c_ref, h_ref, cp_ref, hp_ref, cn_ref, hn_ref, cw_ref, w_ref, u_ref, seqs):
    tm = b_ref.shape[0]
    pos, length = _seq_position(_mixer_tile(), tm, seqs)
    h8 = POOL_HALO
    u_ref[0:h8, :] = (cp_ref[...].astype(F32) * hp_ref[...].astype(F32))[CONV_HALO - h8:, :]
    u_ref[h8:h8 + tm, :] = c_ref[...].astype(F32) * h_ref[...].astype(F32)
    u_ref[h8 + tm:, :] = (cn_ref[...].astype(F32) * hn_ref[...].astype(F32))[0:h8, :]
    prev = jnp.where(pos >= 1, u_ref[h8 - 1:h8 - 1 + tm, :], 0.0)
    nxt = jnp.where(pos + 1 < length, u_ref[h8 + 1:h8 + 1 + tm, :], 0.0)
    conv = prev * cw_ref[0:1, :] + u_ref[h8:h8 + tm, :] * cw_ref[1:2, :] + nxt * cw_ref[2:3, :]
    zin = (b_ref[...].astype(F32) * conv).astype(BF16)
    return jnp.dot(zin, w_ref[...], preferred_element_type=F32)


def _post_pool_kernel(xm_ref, xp_ref, xn_ref, w_ref, sc_ref, x_ref, *rest, seqs):
    *route, carry_ref, y_ref, xs_ref = rest
    _pipelined_post(functools.partial(_pool_mix, xm_ref, xp_ref, xn_ref, w_ref, sc_ref, xs_ref, seqs), x_ref, route,
                    y_ref, carry_ref)


def _pool_mix(xm_ref, xp_ref, xn_ref, w_ref, sc_ref, xs_ref, seqs):
    tm = xm_ref.shape[0]
    pos, length = _seq_position(_mixer_tile(), tm, seqs)
    h8 = POOL_HALO
    xs_ref[0:h8, :] = xp_ref[...]
    xs_ref[h8:h8 + tm, :] = xm_ref[...]
    xs_ref[h8 + tm:, :] = xn_ref[...]
    ys = []
    for g, w in enumerate(POOL_WINDOWS):
        cs = slice(g * POOL_GROUP_DIM, (g + 1) * POOL_GROUP_DIM)
        acc = jnp.zeros((tm, POOL_GROUP_DIM), F32)
        for s in range(-(w // 2), w - w // 2):
            ok = (pos + s >= 0) & (pos + s < length)
            acc = acc + jnp.where(ok, xs_ref[h8 + s:h8 + s + tm, cs], 0.0)
        lo = jnp.maximum(pos - w // 2, 0)
        hi = jnp.minimum(pos - w // 2 + w, length)
        pooled = acc / (hi - lo).astype(F32) - xm_ref[:, cs]
        ys.append(jnp.dot(pooled.astype(BF16), w_ref[g], preferred_element_type=F32))
    return jnp.concatenate(ys, axis=1) * sc_ref[...]


def _route_operands(router_w, router_b, tm):
    rw_t = router_w.T
    rw_hi = rw_t.astype(BF16)
    rw_lo = (rw_t - rw_hi.astype(F32)).astype(BF16)
    tri = jnp.asarray(np.triu(np.ones((tm, tm), np.float32), 1), BF16)
    return jnp.concatenate([rw_hi, rw_lo], axis=0), rw_hi, router_b.reshape(N_EXPERTS, 1).astype(F32), tri


def _post_call(kind, n, mixer_args, mixer_specs, x, lng, lnb, route_ops, scratch, seqs):
    tm = POST_TM
    rw2, rwhi, rb, tri = route_ops
    const = lambda shape: pl.BlockSpec(shape, lambda i: (0,) * len(shape))
    done = lambda i: jnp.maximum(i - 1, 0)
    in_specs = list(mixer_specs) + [
        pl.BlockSpec((tm, D_MODEL), lambda i: (done(i), 0)),
        const((1, D_MODEL)), const((1, D_MODEL)),
        const((2 * N_EXPERTS, D_MODEL)), const((N_EXPERTS, D_MODEL)), const((N_EXPERTS, 1)), const((tm, tm))]
    out_shape = [jax.ShapeDtypeStruct((n, D_MODEL), F32),
                 jax.ShapeDtypeStruct((8, n), I32),
                 jax.ShapeDtypeStruct((8, n), F32),
                 jax.ShapeDtypeStruct((N_EXPERTS, 128), F32)]
    out_specs = [pl.BlockSpec((tm, D_MODEL), lambda i: (done(i), 0)),
                 pl.BlockSpec((8, tm), lambda i: (0, done(i))),
                 pl.BlockSpec((8, tm), lambda i: (0, done(i))),
                 const((N_EXPERTS, 128))]
    body = {"attn": _post_attn_kernel,
            "conv": functools.partial(_post_conv_kernel, seqs=seqs),
            "pool": functools.partial(_post_pool_kernel, seqs=seqs)}[kind]
    return pl.pallas_call(
        body,
        out_shape=out_shape,
        grid=(n // tm + 1,),
        in_specs=in_specs,
        out_specs=out_specs,
        scratch_shapes=[pltpu.VMEM((N_EXPERTS, 128), F32), pltpu.VMEM((tm, D_MODEL), F32)] + list(scratch),
        compiler_params=_cparams(("arbitrary",)),
        name="post_" + kind,
    )(*mixer_args, x, lng, lnb, rw2, rwhi, rb, tri)


def _moe_kernel(bexp, gsrc_ref, sdst_ref, x_hbm, wg_ref, wu_ref, wd_ref, out_hbm, xbuf, ybuf, gsem, ssem):
    del bexp
    groups, sub = ybuf.shape[1], ybuf.shape[2]
    tm = groups * sub
    s = pl.program_id(0)
    last = pl.num_programs(0) - 1
    cur = s % 2
    prev = 1 - cur

    def gather_row(g, u):
        tok = gsrc_ref[0, g * sub + u]
        return pltpu.make_async_copy(x_hbm.at[pl.ds(tok, 1)], xbuf.at[cur, g, pl.ds(u, 1)], gsem.at[cur])

    def scatter_row(g, u):
        row = sdst_ref[0, g * sub + u]
        return pltpu.make_async_copy(ybuf.at[cur, g, pl.ds(u, 1)], out_hbm.at[pl.ds(row, 1)], ssem.at[cur])

    def gather_wait(sl):
        pltpu.make_async_copy(xbuf.at[sl], xbuf.at[sl], gsem.at[sl]).wait()

    def scatter_wait(sl):
        pltpu.make_async_copy(ybuf.at[sl], ybuf.at[sl], ssem.at[sl]).wait()

    @pl.when(s == 0)
    def _():
        ybuf[...] = jnp.zeros_like(ybuf)

    @pl.when(s > 0)
    def _():
        gather_wait(prev)
        scatter_wait(prev)

    def start_rows(g, carry):
        for u in range(sub):
            gather_row(g, u).start()
            scatter_row(g, u).start()
        return carry
    lax.fori_loop(0, groups, start_rows, 0)

    @pl.when((s > 0) & (s < last))
    def _():
        x = xbuf[prev].reshape(tm, D_MODEL).astype(BF16)
        gate = jnp.dot(x, wg_ref[...], preferred_element_type=F32)
        up = jnp.dot(x, wu_ref[...], preferred_element_type=F32)
        hdn = (gate * jax.nn.sigmoid(gate) * up).astype(BF16)
        ybuf[prev] = jnp.dot(hdn, wd_ref[...], preferred_element_type=F32).reshape(groups, sub, D_MODEL)

    @pl.when(s == last)
    def _():
        gather_wait(cur)
        scatter_wait(cur)


def _moe(x1, wg, wu, wd, layer, bexp, gsrc, sdst, n_blocks):
    n = x1.shape[0]
    tm = MOE_TM
    block_of_step = lambda s: jnp.clip(s - 1, 0, n_blocks - 1)
    wspec = lambda shape: pl.BlockSpec((None, None) + shape, lambda s, bexp: (layer, bexp[block_of_step(s)], 0, 0))
    smem_row = lambda lead: pl.BlockSpec((None, 1, tm), lambda s, bexp: (s + lead, 0, 0), memory_space=pltpu.SMEM)
    grid_spec = pltpu.PrefetchScalarGridSpec(
        num_scalar_prefetch=1,
        grid=(n_blocks + 2,),
        in_specs=[smem_row(MOE_LEAD_BLOCKS), smem_row(MOE_LEAD_BLOCKS - 2),
                  pl.BlockSpec(memory_space=pl.ANY),
                  wspec((D_MODEL, D_EXPERT)), wspec((D_MODEL, D_EXPERT)), wspec((D_EXPERT, D_MODEL))],
        out_specs=pl.BlockSpec(memory_space=pl.ANY),
        scratch_shapes=[pltpu.VMEM((2, tm // F32_SUBLANES, F32_SUBLANES, D_MODEL), F32),
                        pltpu.VMEM((2, tm // F32_SUBLANES, F32_SUBLANES, D_MODEL), F32),
                        pltpu.SemaphoreType.DMA((2,)), pltpu.SemaphoreType.DMA((2,))],
    )
    return pl.pallas_call(
        _moe_kernel,
        out_shape=jax.ShapeDtypeStruct((2 * n + tm, D_MODEL), F32),
        grid_spec=grid_spec,
        compiler_params=_cparams(("arbitrary",)),
        name="moe_experts",
    )(bexp, gsrc, sdst, x1, wg, wu, wd)


def _dispatch_tables(ri, cnt, n):
    tm = MOE_TM
    n_blocks = -(-2 * n // tm) + N_EXPERTS
    n_table_blocks = n_blocks + 2 * MOE_LEAD_BLOCKS
    experts = ri[0:2]
    rank = ri[2:4]
    counts = cnt[:, 0].astype(I32)
    nblk = (counts + tm - 1) // tm
    bend = jnp.cumsum(nblk)
    bstart = bend - nblk
    first_block = jnp.sum(jnp.where(experts[:, :, None] == jnp.arange(N_EXPERTS, dtype=I32), bstart, 0), axis=-1)
    dest = (first_block + MOE_LEAD_BLOCKS) * tm + rank
    code = 2 * jnp.arange(n, dtype=I32)[None, :] + jnp.arange(2, dtype=I32)[:, None]
    rows = jnp.arange(n_table_blocks * tm, dtype=I32)
    inv = jnp.full(rows.shape, -1, I32).at[dest.reshape(-1)].set(code.reshape(-1), unique_indices=True)
    gsrc = jnp.where(inv < 0, n - 1, inv >> 1)
    sdst = jnp.where(inv < 0, 2 * n + rows % tm, (inv & 1) * n + (inv >> 1))
    blocks = jnp.arange(n_blocks, dtype=I32)
    bexp_raw = jnp.minimum(jnp.sum(blocks[:, None] >= bend[None, :], axis=1), N_EXPERTS - 1).astype(I32)
    bexp = bexp_raw[jnp.minimum(blocks, bend[-1] - 1)]
    shape3 = (n_table_blocks, 1, tm)
    return bexp, gsrc.reshape(shape3), sdst.reshape(shape3), n_blocks


def _ln2_kernel(x_ref, y0_ref, y1_ref, g_ref, lng_ref, lnb_ref, o_ref, *maybe_ob_ref):
    z = ALPHA * x_ref[...] + y0_ref[...] * g_ref[:, 0:1] + y1_ref[...] * g_ref[:, 1:2]
    mu = jnp.mean(z, axis=-1, keepdims=True)
    zc = z - mu
    var = jnp.mean(zc * zc, axis=-1, keepdims=True)
    out = zc * lax.rsqrt(var + LN_EPS) * lng_ref[...] + lnb_ref[...]
    o_ref[...] = out
    for ob_ref in maybe_ob_ref:
        ob_ref[...] = out.astype(BF16)


def _combine_ln(x1, y2, gates, lng, lnb, *, first_row=0, n_rows=None, with_bf16=True):
    tm = LN2_TM
    n_rows = x1.shape[0] if n_rows is None else n_rows
    assert first_row % tm == 0 and n_rows % tm == 0
    off = first_row // tm
    slot1 = x1.shape[0] // tm
    row = lambda i: (i + off, 0)
    const = lambda i: (0, 0)
    out_shape = [jax.ShapeDtypeStruct((n_rows, D_MODEL), F32)]
    out_specs = [pl.BlockSpec((tm, D_MODEL), lambda i: (i, 0))]
    if with_bf16:
        out_shape.append(jax.ShapeDtypeStruct((n_rows, D_MODEL), BF16))
        out_specs.append(pl.BlockSpec((tm, D_MODEL), lambda i: (i, 0)))
    return pl.pallas_call(
        _ln2_kernel,
        out_shape=out_shape,
        grid=(n_rows // tm,),
        in_specs=[pl.BlockSpec((tm, D_MODEL), row),
                  pl.BlockSpec((tm, D_MODEL), row),
                  pl.BlockSpec((tm, D_MODEL), lambda i: (i + off + slot1, 0)),
                  pl.BlockSpec((tm, 2), row),
                  pl.BlockSpec((1, D_MODEL), const), pl.BlockSpec((1, D_MODEL), const)],
        out_specs=out_specs,
        compiler_params=_cparams(("parallel",)),
        name="combine_ln",
    )(x1, y2, y2, gates, lng, lnb)


def _trunk(x, seqs, attn_w_qkv, attn_w_o, attn_rpb, pool_w, pool_scale, conv_w_in, conv_w, conv_w_out,
           router_w, router_b, expert_w_gate, expert_w_up, expert_w_down, ln_g, ln_b):
    n = x.shape[0]
    tm = POST_TM
    assert n % MM_TM == 0 and n % LN2_TM == 0 and all(l % tm == 0 for _, l in seqs)
    route_ops = _route_operands(router_w, router_b, tm)
    attn_tables = _attn_tables(seqs)
    xb = x.astype(BF16)
    wg_all, wu_all, wd_all = (w.astype(BF16) for w in (expert_w_gate, expert_w_up, expert_w_down))
    mixer_tile = lambda i: jnp.minimum(i, n // tm - 1)
    halo_before = lambda halo: (lambda i: jnp.maximum(mixer_tile(i) * (tm // halo) - 1, 0))
    halo_after = lambda halo: (lambda i: jnp.minimum((mixer_tile(i) + 1) * (tm // halo), n // halo - 1))
    tile_spec = lambda col=0: pl.BlockSpec((tm, D_MODEL), lambda i: (mixer_tile(i), col))
    const2 = lambda i: (0, 0)
    for layer in range(DEPTH):
        kind, slot = layer % N_MIXERS, layer // N_MIXERS
        lng1, lnb1 = ln_g[layer, 0].reshape(1, D_MODEL), ln_b[layer, 0].reshape(1, D_MODEL)
        if kind == 0:
            qkv = _project(xb, attn_w_qkv[slot].astype(BF16), n_scaled=D_MODEL // MM_TN, scale=HEAD_DIM ** -0.5,
                           name="qkv_proj")
            o = _attention(qkv, _attn_bias(attn_rpb[slot]), attn_tables)
            args = (o, attn_w_o[slot].astype(BF16))
            specs = (tile_spec(), pl.BlockSpec((D_MODEL, D_MODEL), const2))
            post = _post_call("attn", n, args, specs, x, lng1, lnb1, route_ops, (), seqs)
        elif kind == 1:
            before, after = halo_before(POOL_HALO), halo_after(POOL_HALO)
            args = (x, x, x, pool_w[slot].astype(BF16), pool_scale[slot].reshape(1, D_MODEL))
            specs = (tile_spec(),
                     pl.BlockSpec((POOL_HALO, D_MODEL), lambda i: (before(i), 0)),
                     pl.BlockSpec((POOL_HALO, D_MODEL), lambda i: (after(i), 0)),
                     pl.BlockSpec((len(POOL_WINDOWS), POOL_GROUP_DIM, POOL_GROUP_DIM), lambda i: (0, 0, 0)),
                     pl.BlockSpec((1, D_MODEL), const2))
            scratch = (pltpu.VMEM((tm + 2 * POOL_HALO, D_MODEL), F32),)
            post = _post_call("pool", n, args, specs, x, lng1, lnb1, route_ops, scratch, seqs)
        else:
            bch = _project(xb, conv_w_in[slot].astype(BF16), name="conv_in_proj")
            before, after = halo_before(CONV_HALO), halo_after(CONV_HALO)
            prev = lambda c: pl.BlockSpec((CONV_HALO, D_MODEL), lambda i, c=c: (before(i), c))
            nxt = lambda c: pl.BlockSpec((CONV_HALO, D_MODEL), lambda i, c=c: (after(i), c))
            args = (bch,) * 7 + (conv_w[slot], conv_w_out[slot].astype(BF16))
            specs = (tile_spec(0), tile_spec(1), tile_spec(2),
                     prev(1), prev(2), nxt(1), nxt(2),
                     pl.BlockSpec((3, D_MODEL), const2),
                     pl.BlockSpec((D_MODEL, D_MODEL), const2))
            scratch = (pltpu.VMEM((tm + 2 * POOL_HALO, D_MODEL), F32),)
            post = _post_call("conv", n, args, specs, x, lng1, lnb1, route_ops, scratch, seqs)
        x1, ri, rg, cnt = post
        bexp, gsrc, sdst, n_blocks = _dispatch_tables(ri, cnt, n)
        y2 = _moe(x1, wg_all, wu_all, wd_all, layer, bexp, gsrc, sdst, n_blocks)
        gates = rg[0:2].T
        lng2, lnb2 = ln_g[layer, 1].reshape(1, D_MODEL), ln_b[layer, 1].reshape(1, D_MODEL)
        if layer + 1 < DEPTH:
            x, xb = _combine_ln(x1, y2, gates, lng2, lnb2)
    n0 = seqs[0][0] * seqs[0][1]
    (y_first,) = _combine_ln(x1, y2, gates, lng2, lnb2, first_row=0, n_rows=n0, with_bf16=False)
    (y_second,) = _combine_ln(x1, y2, gates, lng2, lnb2, first_row=n0, n_rows=n - n0, with_bf16=False)
    return y_first, y_second


def kernel(x_prompt, x_sample, attn_w_qkv, attn_w_o, attn_rpb, pool_w, pool_scale, conv_w_in, conv_w, conv_w_out,
           router_w, router_b, expert_w_gate, expert_w_up, expert_w_down, ln_g, ln_b):
    seqs = (x_prompt.shape[:2], x_sample.shape[:2])
    x = jnp.concatenate([x_prompt.reshape(-1, D_MODEL), x_sample.reshape(-1, D_MODEL)], axis=0)
    y_prompt, y_sample = _trunk(x, seqs, attn_w_qkv, attn_w_o, attn_rpb, pool_w, pool_scale, conv_w_in, conv_w,
                                conv_w_out, router_w, router_b, expert_w_gate, expert_w_up, expert_w_down, ln_g, ln_b)
    return y_prompt.reshape(x_prompt.shape), y_sample.reshape(x_sample.shape)
```

```python
import functools

import numpy as np
import jax
import jax.numpy as jnp
from jax import lax
from jax.experimental import pallas as pl
from jax.experimental.pallas import tpu as pltpu

F32 = jnp.float32
BF16 = jnp.bfloat16
I32 = jnp.int32
U32 = jnp.uint32

D_MODEL = 2048
DEPTH = 4
GRID_W = 64
N_MIXERS = 3
N_HEADS = 16
HEAD_DIM = D_MODEL // N_HEADS
WIN_ROWS = 8
WIN_COLS = 16
COL_BLOCK = 16
KEY_COLS = 2 * COL_BLOCK
N_COL_BLOCKS = GRID_W // COL_BLOCK
POOL_WINDOWS = (2, 4, 8, 16)
POOL_GROUP_DIM = D_MODEL // len(POOL_WINDOWS)
POOL_HALO = 8
CONV_HALO = 16
N_EXPERTS = 16
N_EXPERT_GROUPS = 4
EXPERTS_PER_GROUP = N_EXPERTS // N_EXPERT_GROUPS
D_EXPERT = D_MODEL // 2
ALPHA = (2 * DEPTH) ** 0.25
LN_EPS = 1e-5
MASK_VALUE = -1e30

MM_TM = 1024
MM_TN = 1024
ATT_ROWS = 8
ATT_TOK = ATT_ROWS * GRID_W
ATT_KROWS = 16
ATT_KSUB_ROWS = 4
ATT_KSUB_TOK = ATT_KSUB_ROWS * GRID_W
ATT_N_KSUB = ATT_KROWS // ATT_KSUB_ROWS
ATT_HEADS_PER_STEP = 4
POST_TM = 256
MOE_TM = 256
F32_SUBLANES = 8
MOE_LEAD_BLOCKS = 2
LN2_TM = 512
VMEM_LIMIT = 56 * 1024 * 1024


def _cparams(sem):
    return pltpu.CompilerParams(dimension_semantics=sem, vmem_limit_bytes=VMEM_LIMIT)


def _mm_kernel(x_ref, w_ref, o_ref, *, n_scaled, scale):
    acc = jnp.dot(x_ref[...], w_ref[...], preferred_element_type=F32)
    if n_scaled:
        acc = acc * jnp.where(pl.program_id(1) < n_scaled, F32(scale), F32(1.0))
    o_ref[...] = acc.astype(o_ref.dtype)


def _project(x, w, *, n_scaled=0, scale=1.0, name):
    m, k = x.shape
    n = w.shape[1]
    assert m % MM_TM == 0 and n % MM_TN == 0
    return pl.pallas_call(
        functools.partial(_mm_kernel, n_scaled=n_scaled, scale=scale),
        out_shape=jax.ShapeDtypeStruct((m, n), BF16),
        grid=(m // MM_TM, n // MM_TN),
        in_specs=[pl.BlockSpec((MM_TM, k), lambda i, j: (i, 0)),
                  pl.BlockSpec((k, MM_TN), lambda i, j: (0, j))],
        out_specs=pl.BlockSpec((MM_TM, MM_TN), lambda i, j: (i, j)),
        compiler_params=_cparams(("parallel", "arbitrary")),
        name=name,
    )(x, w)


def _key_col_start(j):
    return int(np.clip(j * COL_BLOCK - WIN_COLS // 2, 0, GRID_W - KEY_COLS))


def _attn_tables(seqs):
    blocks = []
    row0 = 0
    for batch, length in seqs:
        rows = length // GRID_W
        assert rows % ATT_ROWS == 0 and rows >= ATT_KROWS
        for _ in range(batch):
            for r0 in range(0, rows, ATT_ROWS):
                variant = 0 if r0 == 0 else (2 if r0 == rows - ATT_ROWS else 1)
                ks = row0 + int(np.clip(r0 - WIN_ROWS // 2, 0, rows - ATT_KROWS))
                blocks.append((variant, (row0 + r0) // ATT_ROWS, ks // ATT_KSUB_ROWS))
            row0 += rows
    blocks.sort(key=lambda b: b[0])
    var = np.array([b[0] for b in blocks], np.int32)
    gtab = np.array([b[1] for b in blocks], np.int32)
    kwin = np.array([b[2] for b in blocks], np.int32)
    return gtab, kwin, var


def _attn_bias(rpb):
    i = np.arange(ATT_ROWS)[:, None, None, None]
    cq = np.arange(COL_BLOCK)[None, :, None, None]
    rk = np.arange(ATT_KROWS)[None, None, :, None]
    ck = np.arange(KEY_COLS)[None, None, None, :]
    n_dy, n_dx = 2 * WIN_ROWS - 1, 2 * WIN_COLS - 1
    sel_dy = np.zeros((3, n_dy, ATT_ROWS, ATT_KROWS), np.float32)
    row_ok = np.zeros((3, ATT_ROWS, ATT_KROWS), bool)
    for variant in range(3):
        if variant == 0:
            rel_start = np.maximum(i - WIN_ROWS // 2, 0)
            dy = rk - i + WIN_ROWS - 1
        elif variant == 1:
            rel_start = i
            dy = rk - i + WIN_ROWS - 1 - WIN_ROWS // 2
        else:
            rel_start = np.minimum(i + WIN_ROWS // 2, ATT_KROWS - WIN_ROWS)
            dy = rk - i + WIN_ROWS - 1 - (ATT_KROWS - ATT_ROWS)
        row_ok[variant] = ((rk >= rel_start) & (rk < rel_start + WIN_ROWS))[:, 0, :, 0]
        dyc = np.clip(dy, 0, n_dy - 1)[:, 0, :, 0]
        sel_dy[variant] = np.arange(n_dy)[:, None, None] == dyc[None]
    sel_dx = np.zeros((N_COL_BLOCKS, n_dx, COL_BLOCK, KEY_COLS), np.float32)
    col_ok = np.zeros((N_COL_BLOCKS, COL_BLOCK, KEY_COLS), bool)
    for j in range(N_COL_BLOCKS):
        qc = j * COL_BLOCK + cq
        kc = _key_col_start(j) + ck
        c_start = np.clip(qc - WIN_COLS // 2, 0, GRID_W - WIN_COLS)
        col_ok[j] = ((kc >= c_start) & (kc < c_start + WIN_COLS))[0, :, 0, :]
        dxc = np.clip(kc - qc + WIN_COLS - 1, 0, n_dx - 1)[0, :, 0, :]
        sel_dx[j] = np.arange(n_dx)[:, None, None] == dxc[None]
    hp = lax.Precision.HIGHEST
    by = jnp.einsum("hab,vair->vhirb", rpb.astype(F32), jnp.asarray(sel_dy), precision=hp)
    bias = jnp.einsum("vhirb,jbck->vhjicrk", by, jnp.asarray(sel_dx), precision=hp)
    ok = row_ok[:, None, None, :, None, :, None] & col_ok[None, None, :, None, :, None, :]
    bias = jnp.where(jnp.asarray(ok), bias, F32(MASK_VALUE))
    return bias.reshape(3, N_HEADS, N_COL_BLOCKS, ATT_ROWS * COL_BLOCK, ATT_KROWS * KEY_COLS)


def _attn_kernel(gtab, kwin, var, q_ref, k0, k1, k2, k3, v0, v1, v2, v3, bias_ref, o_ref, kf_ref, vf_ref):
    del gtab, kwin, var
    krefs = (k0, k1, k2, k3)
    vrefs = (v0, v1, v2, v3)
    nt = (((1,), (1,)), ((), ()))
    lanes = [slice(hh * HEAD_DIM, (hh + 1) * HEAD_DIM) for hh in range(ATT_HEADS_PER_STEP)]
    work = [(hh, j) for hh in range(ATT_HEADS_PER_STEP) for j in range(N_COL_BLOCKS)]

    def window(ref, hh, j):
        kc0 = _key_col_start(j)
        return jnp.concatenate(
            [ref[hh, rk * GRID_W + kc0:rk * GRID_W + kc0 + KEY_COLS, :] for rk in range(ATT_KROWS)],
            axis=0).astype(BF16)

    for hh in range(ATT_HEADS_PER_STEP):
        for i in range(ATT_N_KSUB):
            kf_ref[hh, i * ATT_KSUB_TOK:(i + 1) * ATT_KSUB_TOK, :] = krefs[i][:, lanes[hh]].astype(F32)
            vf_ref[hh, i * ATT_KSUB_TOK:(i + 1) * ATT_KSUB_TOK, :] = vrefs[i][:, lanes[hh]].astype(F32)
    scores = []
    for hh, j in work:
        qj = jnp.concatenate(
            [q_ref[rr * GRID_W + j * COL_BLOCK:rr * GRID_W + (j + 1) * COL_BLOCK, lanes[hh]]
             for rr in range(ATT_ROWS)], axis=0)
        scores.append(lax.dot_general(qj, window(kf_ref, hh, j), nt, preferred_element_type=F32) + bias_ref[hh, j])
    probs = []
    for s in scores:
        m = jnp.max(s, axis=-1, keepdims=True)
        p = jnp.exp(s - m)
        probs.append((p.astype(BF16), jnp.sum(p, axis=-1, keepdims=True)))
    for (hh, j), (p, denom) in zip(work, probs):
        o = jnp.dot(p, window(vf_ref, hh, j), preferred_element_type=F32) / denom
        ob = o.astype(BF16)
        for rr in range(ATT_ROWS):
            o_ref[rr * GRID_W + j * COL_BLOCK:rr * GRID_W + (j + 1) * COL_BLOCK, lanes[hh]] = \
                ob[rr * COL_BLOCK:(rr + 1) * COL_BLOCK, :]


def _attention(qkv, bias, tables):
    n = qkv.shape[0]
    gtab, kwin, var = tables
    hw = ATT_HEADS_PER_STEP * HEAD_DIM
    n_hsteps = N_HEADS // ATT_HEADS_PER_STEP
    kv_specs = []
    for part in (1, 2):
        for i in range(ATT_N_KSUB):
            kv_specs.append(pl.BlockSpec(
                (ATT_KSUB_TOK, hw),
                lambda hp, s, gtab, kwin, var, i=i, part=part: (kwin[s] + i, part * n_hsteps + hp)))
    grid_spec = pltpu.PrefetchScalarGridSpec(
        num_scalar_prefetch=3,
        grid=(n_hsteps, len(gtab)),
        in_specs=[pl.BlockSpec((ATT_TOK, hw), lambda hp, s, gtab, kwin, var: (gtab[s], hp))] + kv_specs + [
            pl.BlockSpec((None, ATT_HEADS_PER_STEP, N_COL_BLOCKS, ATT_ROWS * COL_BLOCK, ATT_KROWS * KEY_COLS),
                         lambda hp, s, gtab, kwin, var: (var[s], hp, 0, 0, 0))],
        out_specs=pl.BlockSpec((ATT_TOK, hw), lambda hp, s, gtab, kwin, var: (gtab[s], hp)),
        scratch_shapes=[pltpu.VMEM((ATT_HEADS_PER_STEP, ATT_KROWS * GRID_W, HEAD_DIM), F32),
                        pltpu.VMEM((ATT_HEADS_PER_STEP, ATT_KROWS * GRID_W, HEAD_DIM), F32)],
    )
    return pl.pallas_call(
        _attn_kernel,
        out_shape=jax.ShapeDtypeStruct((n, D_MODEL), BF16),
        grid_spec=grid_spec,
        compiler_params=_cparams(("parallel", "arbitrary")),
        name="nbr_attention",
    )(jnp.asarray(gtab), jnp.asarray(kwin), jnp.asarray(var), qkv, *([qkv] * (2 * ATT_N_KSUB)), bias)


def _first_max4(vals):
    best = vals[0]
    idx = jnp.zeros(best.shape, I32)
    for i in range(1, 4):
        take = vals[i] > best
        best = jnp.where(take, vals[i], best)
        idx = jnp.where(take, I32(i), idx)
    return idx


def _select4(idx, vals):
    return jnp.where(idx == 0, vals[0], jnp.where(idx == 1, vals[1], jnp.where(idx == 2, vals[2], vals[3])))


def _ln_route(z, counted, lng_ref, lnb_ref, rw2_ref, rwhi_ref, rb_ref, tri_ref, x1_ref, ri_ref, rg_ref, cnt_ref,
              carry_ref):
    tm = z.shape[0]

    mu = jnp.mean(z, axis=-1, keepdims=True)
    zc = z - mu
    var = jnp.mean(zc * zc, axis=-1, keepdims=True)
    x1 = zc * lax.rsqrt(var + LN_EPS) * lng_ref[...] + lnb_ref[...]
    x1_ref[...] = x1

    hi = x1.astype(BF16)
    lo = (x1 - hi.astype(F32)).astype(BF16)
    nt = (((1,), (1,)), ((), ()))
    l2 = lax.dot_general(rw2_ref[...], hi, nt, preferred_element_type=F32)
    l1 = lax.dot_general(rwhi_ref[...], lo, nt, preferred_element_type=F32)
    logits = l2[:N_EXPERTS] + l2[N_EXPERTS:] + l1
    ex = jnp.exp(logits - jnp.max(logits, axis=0, keepdims=True))
    scores = ex / jnp.sum(ex, axis=0, keepdims=True)
    biased = scores + rb_ref[...]
    sr = [scores[e:e + 1, :] for e in range(N_EXPERTS)]
    br = [biased[e:e + 1, :] for e in range(N_EXPERTS)]

    group_scores = []
    for g in range(N_EXPERT_GROUPS):
        a, b, c, d = br[g * EXPERTS_PER_GROUP:(g + 1) * EXPERTS_PER_GROUP]
        hi1, lo1, hi2, lo2 = jnp.maximum(a, b), jnp.minimum(a, b), jnp.maximum(c, d), jnp.minimum(c, d)
        group_scores.append(jnp.maximum(hi1, hi2) + jnp.maximum(jnp.minimum(hi1, hi2), jnp.maximum(lo1, lo2)))
    gsel = _first_max4(group_scores)
    vb = [_select4(gsel, [br[g * EXPERTS_PER_GROUP + i] for g in range(N_EXPERT_GROUPS)]) for i in range(4)]
    vs = [_select4(gsel, [sr[g * EXPERTS_PER_GROUP + i] for g in range(N_EXPERT_GROUPS)]) for i in range(4)]
    i1 = _first_max4(vb)
    i2 = _first_max4([jnp.where(i1 == i, -jnp.inf, vb[i]) for i in range(4)])
    e0 = gsel * EXPERTS_PER_GROUP + i1
    e1 = gsel * EXPERTS_PER_GROUP + i2
    w0 = _select4(i1, vs)
    w1 = _select4(i2, vs)
    den = w0 + w1

    eio = lax.broadcasted_iota(I32, (N_EXPERTS, tm), 0)
    oh0 = eio == e0
    oh1 = eio == e1
    ohs = jnp.where(oh0 | oh1, F32(1.0), F32(0.0))
    carry = carry_ref[:, 0:1]
    before = jnp.dot(ohs.astype(BF16), tri_ref[...], preferred_element_type=F32) + carry
    r0 = jnp.sum(jnp.where(oh0, before, 0.0), axis=0, keepdims=True)
    r1 = jnp.sum(jnp.where(oh1, before, 0.0), axis=0, keepdims=True)
    tile_counts = jnp.where(counted, jnp.sum(ohs, axis=1, keepdims=True), 0.0)
    carry_ref[...] = jnp.broadcast_to(carry + tile_counts, carry_ref.shape)
    cnt_ref[...] = carry_ref[...]
    ri_ref[...] = jnp.concatenate([e0, e1, r0.astype(I32), r1.astype(I32), jnp.zeros((4, tm), I32)], axis=0)
    rg_ref[...] = jnp.concatenate([w0 / den, w1 / den, jnp.zeros((6, tm), F32)], axis=0)


def _seq_position(tile, tm, seqs):
    (b0, l0), (_, l1) = seqs
    start = tile * tm
    in_first = start < b0 * l0
    length = jnp.where(in_first, I32(l0), I32(l1))
    pos0 = jnp.where(in_first, start % l0, (start - b0 * l0) % l1)
    return pos0 + lax.broadcasted_iota(I32, (tm, 1), 0), length


def _mixer_tile():
    return jnp.minimum(pl.program_id(0), pl.num_programs(0) - 2)


def _pipelined_post(mix_fn, x_ref, route, y_ref, carry_ref):
    i = pl.program_id(0)

    @pl.when(i == 0)
    def _():
        y_ref[...] = jnp.zeros_like(y_ref)
        carry_ref[...] = jnp.zeros_like(carry_ref)

    z = ALPHA * x_ref[...] + y_ref[...]
    y_ref[...] = mix_fn()
    _ln_route(z, i > 0, *route, carry_ref)


def _post_attn_kernel(o_ref, w_ref, x_ref, *rest):
    *route, carry_ref, y_ref = rest
    _pipelined_post(lambda: jnp.dot(o_ref[...], w_ref[...], preferred_element_type=F32), x_ref, route, y_ref,
                    carry_ref)


def _post_conv_kernel(b_ref, c_ref, h_ref, cp_ref, hp_ref, cn_ref, hn_ref, cw_ref, w_ref, x_ref, *rest, seqs):
    *route, carry_ref, y_ref, u_ref = rest
    _pipelined_post(functools.partial(_conv_mix, b_ref, c_ref, h_ref, cp_ref, hp_ref, cn_ref, hn_ref, cw_ref, w_ref,
                                      u_ref, seqs), x_ref, route, y_ref, carry_ref)


def _conv_mix(b_ref, c_ref, h_ref, cp_ref, hp_ref, cn_ref, hn_ref, cw_ref, w_ref, u_ref, seqs):
    tm = b_ref.shape[0]
    pos, length = _seq_position(_mixer_tile(), tm, seqs)
    h8 = POOL_HALO
    u_ref[0:h8, :] = (cp_ref[...].astype(F32) * hp_ref[...].astype(F32))[CONV_HALO - h8:, :]
    u_ref[h8:h8 + tm, :] = c_ref[...].astype(F32) * h_ref[...].astype(F32)
    u_ref[h8 + tm:, :] = (cn_ref[...].astype(F32) * hn_ref[...].astype(F32))[0:h8, :]
    prev = jnp.where(pos >= 1, u_ref[h8 - 1:h8 - 1 + tm, :], 0.0)
    nxt = jnp.where(pos + 1 < length, u_ref[h8 + 1:h8 + 1 + tm, :], 0.0)
    conv = prev * cw_ref[0:1, :] + u_ref[h8:h8 + tm, :] * cw_ref[1:2, :] + nxt * cw_ref[2:3, :]
    zin = (b_ref[...].astype(F32) * conv).astype(BF16)
    return jnp.dot(zin, w_ref[...], preferred_element_type=F32)


def _post_pool_kernel(xm_ref, xp_ref, xn_ref, w_ref, sc_ref, x_ref, *rest, seqs):
    *route, carry_ref, y_ref, xs_ref = rest
    _pipelined_post(functools.partial(_pool_mix, xm_ref, xp_ref, xn_ref, w_ref, sc_ref, xs_ref, seqs), x_ref, route,
                    y_ref, carry_ref)


def _pool_mix(xm_ref, xp_ref, xn_ref, w_ref, sc_ref, xs_ref, seqs):
    tm = xm_ref.shape[0]
    pos, length = _seq_position(_mixer_tile(), tm, seqs)
    h8 = POOL_HALO
    xs_ref[0:h8, :] = xp_ref[...]
    xs_ref[h8:h8 + tm, :] = xm_ref[...]
    xs_ref[h8 + tm:, :] = xn_ref[...]
    ys = []
    for g, w in enumerate(POOL_WINDOWS):
        cs = slice(g * POOL_GROUP_DIM, (g + 1) * POOL_GROUP_DIM)
        acc = jnp.zeros((tm, POOL_GROUP_DIM), F32)
        for s in range(-(w // 2), w - w // 2):
            ok = (pos + s >= 0) & (pos + s < length)
            acc = acc + jnp.where(ok, xs_ref[h8 + s:h8 + s + tm, cs], 0.0)
        lo = jnp.maximum(pos - w // 2, 0)
        hi = jnp.minimum(pos - w // 2 + w, length)
        pooled = acc / (hi - lo).astype(F32) - xm_ref[:, cs]
        ys.append(jnp.dot(pooled.astype(BF16), w_ref[g], preferred_element_type=F32))
    return jnp.concatenate(ys, axis=1) * sc_ref[...]


def _route_operands(router_w, router_b, tm):
    rw_t = router_w.T
    rw_hi = rw_t.astype(BF16)
    rw_lo = (rw_t - rw_hi.astype(F32)).astype(BF16)
    tri = jnp.asarray(np.triu(np.ones((tm, tm), np.float32), 1), BF16)
    return jnp.concatenate([rw_hi, rw_lo], axis=0), rw_hi, router_b.reshape(N_EXPERTS, 1).astype(F32), tri


def _post_call(kind, n, mixer_args, mixer_specs, x, lng, lnb, route_ops, scratch, seqs):
    tm = POST_TM
    rw2, rwhi, rb, tri = route_ops
    const = lambda shape: pl.BlockSpec(shape, lambda i: (0,) * len(shape))
    done = lambda i: jnp.maximum(i - 1, 0)
    in_specs = list(mixer_specs) + [
        pl.BlockSpec((tm, D_MODEL), lambda i: (done(i), 0)),
        const((1, D_MODEL)), const((1, D_MODEL)),
        const((2 * N_EXPERTS, D_MODEL)), const((N_EXPERTS, D_MODEL)), const((N_EXPERTS, 1)), const((tm, tm))]
    out_shape = [jax.ShapeDtypeStruct((n, D_MODEL), F32),
                 jax.ShapeDtypeStruct((8, n), I32),
                 jax.ShapeDtypeStruct((8, n), F32),
                 jax.ShapeDtypeStruct((N_EXPERTS, 128), F32)]
    out_specs = [pl.BlockSpec((tm, D_MODEL), lambda i: (done(i), 0)),
                 pl.BlockSpec((8, tm), lambda i: (0, done(i))),
                 pl.BlockSpec((8, tm), lambda i: (0, done(i))),
                 const((N_EXPERTS, 128))]
    body = {"attn": _post_attn_kernel,
            "conv": functools.partial(_post_conv_kernel, seqs=seqs),
            "pool": functools.partial(_post_pool_kernel, seqs=seqs)}[kind]
    return pl.pallas_call(
        body,
        out_shape=out_shape,
        grid=(n // tm + 1,),
        in_specs=in_specs,
        out_specs=out_specs,
        scratch_shapes=[pltpu.VMEM((N_EXPERTS, 128), F32), pltpu.VMEM((tm, D_MODEL), F32)] + list(scratch),
        compiler_params=_cparams(("arbitrary",)),
        name="post_" + kind,
    )(*mixer_args, x, lng, lnb, rw2, rwhi, rb, tri)


def _pack_bf16_pairs(y):
    half = y.shape[1] // 2
    lo = pltpu.bitcast(y[:, :half].astype(BF16).astype(F32), U32)
    hi = pltpu.bitcast(y[:, half:].astype(BF16).astype(F32), U32)
    return (lo >> 16) | (hi & jnp.uint32(0xFFFF0000))


def _unpack_bf16_pairs(p):
    lo = pltpu.bitcast(p << 16, F32)
    hi = pltpu.bitcast(p & jnp.uint32(0xFFFF0000), F32)
    return jnp.concatenate([lo, hi], axis=1)


def _moe_kernel(bexp, gsrc_ref, sdst_ref, x_hbm, wg_ref, wu_ref, wd_ref, out_hbm, xbuf, ybuf, gsem, ssem):
    del bexp
    groups, sub = ybuf.shape[1], ybuf.shape[2]
    tm = groups * sub
    s = pl.program_id(0)
    last = pl.num_programs(0) - 1
    cur = s % 2
    prev = 1 - cur

    def gather_row(g, u):
        tok = gsrc_ref[0, g * sub + u]
        return pltpu.make_async_copy(x_hbm.at[pl.ds(tok, 1)], xbuf.at[cur, g, pl.ds(u, 1)], gsem.at[cur])

    def scatter_row(g, u):
        row = sdst_ref[0, g * sub + u]
        return pltpu.make_async_copy(ybuf.at[cur, g, pl.ds(u, 1)], out_hbm.at[pl.ds(row, 1)], ssem.at[cur])

    def gather_wait(sl):
        pltpu.make_async_copy(xbuf.at[sl], xbuf.at[sl], gsem.at[sl]).wait()

    def scatter_wait(sl):
        pltpu.make_async_copy(ybuf.at[sl], ybuf.at[sl], ssem.at[sl]).wait()

    @pl.when(s == 0)
    def _():
        ybuf[...] = jnp.zeros_like(ybuf)

    @pl.when(s > 0)
    def _():
        gather_wait(prev)
        scatter_wait(prev)

    def start_rows(g, carry):
        for u in range(sub):
            gather_row(g, u).start()
            scatter_row(g, u).start()
        return carry
    lax.fori_loop(0, groups, start_rows, 0)

    @pl.when((s > 0) & (s < last))
    def _():
        x = xbuf[prev].reshape(tm, D_MODEL).astype(BF16)
        gate = jnp.dot(x, wg_ref[...], preferred_element_type=F32)
        up = jnp.dot(x, wu_ref[...], preferred_element_type=F32)
        hdn = (gate * jax.nn.sigmoid(gate) * up).astype(BF16)
        y = jnp.dot(hdn, wd_ref[...], preferred_element_type=F32)
        ybuf[prev] = _pack_bf16_pairs(y).reshape(groups, sub, D_MODEL // 2)

    @pl.when(s == last)
    def _():
        gather_wait(cur)
        scatter_wait(cur)


def _moe(x1, wg, wu, wd, layer, bexp, gsrc, sdst, n_blocks):
    n = x1.shape[0]
    tm = MOE_TM
    block_of_step = lambda s: jnp.clip(s - 1, 0, n_blocks - 1)
    wspec = lambda shape: pl.BlockSpec((None, None) + shape, lambda s, bexp: (layer, bexp[block_of_step(s)], 0, 0))
    smem_row = lambda lead: pl.BlockSpec((None, 1, tm), lambda s, bexp: (s + lead, 0, 0), memory_space=pltpu.SMEM)
    grid_spec = pltpu.PrefetchScalarGridSpec(
        num_scalar_prefetch=1,
        grid=(n_blocks + 2,),
        in_specs=[smem_row(MOE_LEAD_BLOCKS), smem_row(MOE_LEAD_BLOCKS - 2),
                  pl.BlockSpec(memory_space=pl.ANY),
                  wspec((D_MODEL, D_EXPERT)), wspec((D_MODEL, D_EXPERT)), wspec((D_EXPERT, D_MODEL))],
        out_specs=pl.BlockSpec(memory_space=pl.ANY),
        scratch_shapes=[pltpu.VMEM((2, tm // F32_SUBLANES, F32_SUBLANES, D_MODEL), F32),
                        pltpu.VMEM((2, tm // F32_SUBLANES, F32_SUBLANES, D_MODEL // 2), U32),
                        pltpu.SemaphoreType.DMA((2,)), pltpu.SemaphoreType.DMA((2,))],
    )
    return pl.pallas_call(
        _moe_kernel,
        out_shape=jax.ShapeDtypeStruct((2 * n + tm, D_MODEL // 2), U32),
        grid_spec=grid_spec,
        compiler_params=_cparams(("arbitrary",)),
        name="moe_experts",
    )(bexp, gsrc, sdst, x1, wg, wu, wd)


def _dispatch_tables(ri, cnt, n):
    tm = MOE_TM
    n_blocks = -(-2 * n // tm) + N_EXPERTS
    n_table_blocks = n_blocks + 2 * MOE_LEAD_BLOCKS
    experts = ri[0:2]
    rank = ri[2:4]
    counts = cnt[:, 0].astype(I32)
    nblk = (counts + tm - 1) // tm
    bend = jnp.cumsum(nblk)
    bstart = bend - nblk
    first_block = jnp.sum(jnp.where(experts[:, :, None] == jnp.arange(N_EXPERTS, dtype=I32), bstart, 0), axis=-1)
    dest = (first_block + MOE_LEAD_BLOCKS) * tm + rank
    code = 2 * jnp.arange(n, dtype=I32)[None, :] + jnp.arange(2, dtype=I32)[:, None]
    rows = jnp.arange(n_table_blocks * tm, dtype=I32)
    inv = jnp.full(rows.shape, -1, I32).at[dest.reshape(-1)].set(code.reshape(-1), unique_indices=True)
    gsrc = jnp.where(inv < 0, n - 1, inv >> 1)
    sdst = jnp.where(inv < 0, 2 * n + rows % tm, (inv & 1) * n + (inv >> 1))
    blocks = jnp.arange(n_blocks, dtype=I32)
    bexp_raw = jnp.minimum(jnp.sum(blocks[:, None] >= bend[None, :], axis=1), N_EXPERTS - 1).astype(I32)
    bexp = bexp_raw[jnp.minimum(blocks, bend[-1] - 1)]
    shape3 = (n_table_blocks, 1, tm)
    return bexp, gsrc.reshape(shape3), sdst.reshape(shape3), n_blocks


def _ln2_kernel(x_ref, y0_ref, y1_ref, g_ref, lng_ref, lnb_ref, o_ref, *maybe_ob_ref):
    z = (ALPHA * x_ref[...] + _unpack_bf16_pairs(y0_ref[...]) * g_ref[:, 0:1]
         + _unpack_bf16_pairs(y1_ref[...]) * g_ref[:, 1:2])
    mu = jnp.mean(z, axis=-1, keepdims=True)
    zc = z - mu
    var = jnp.mean(zc * zc, axis=-1, keepdims=True)
    out = zc * lax.rsqrt(var + LN_EPS) * lng_ref[...] + lnb_ref[...]
    o_ref[...] = out
    for ob_ref in maybe_ob_ref:
        ob_ref[...] = out.astype(BF16)


def _combine_ln(x1, y2, gates, lng, lnb, *, first_row=0, n_rows=None, with_bf16=True):
    tm = LN2_TM
    n_rows = x1.shape[0] if n_rows is None else n_rows
    assert first_row % tm == 0 and n_rows % tm == 0
    off = first_row // tm
    slot1 = x1.shape[0] // tm
    row = lambda i: (i + off, 0)
    const = lambda i: (0, 0)
    out_shape = [jax.ShapeDtypeStruct((n_rows, D_MODEL), F32)]
    out_specs = [pl.BlockSpec((tm, D_MODEL), lambda i: (i, 0))]
    if with_bf16:
        out_shape.append(jax.ShapeDtypeStruct((n_rows, D_MODEL), BF16))
        out_specs.append(pl.BlockSpec((tm, D_MODEL), lambda i: (i, 0)))
    return pl.pallas_call(
        _ln2_kernel,
        out_shape=out_shape,
        grid=(n_rows // tm,),
        in_specs=[pl.BlockSpec((tm, D_MODEL), row),
                  pl.BlockSpec((tm, D_MODEL // 2), row),
                  pl.BlockSpec((tm, D_MODEL // 2), lambda i: (i + off + slot1, 0)),
                  pl.BlockSpec((tm, 2), row),
                  pl.BlockSpec((1, D_MODEL), const), pl.BlockSpec((1, D_MODEL), const)],
        out_specs=out_specs,
        compiler_params=_cparams(("parallel",)),
        name="combine_ln",
    )(x1, y2, y2, gates, lng, lnb)


def _trunk(x, seqs, attn_w_qkv, attn_w_o, attn_rpb, pool_w, pool_scale, conv_w_in, conv_w, conv_w_out,
           router_w, router_b, expert_w_gate, expert_w_up, expert_w_down, ln_g, ln_b):
    n = x.shape[0]
    tm = POST_TM
    assert n % MM_TM == 0 and n % LN2_TM == 0 and all(l % tm == 0 for _, l in seqs)
    route_ops = _route_operands(router_w, router_b, tm)
    attn_tables = _attn_tables(seqs)
    xb = x.astype(BF16)
    wg_all, wu_all, wd_all = (w.astype(BF16) for w in (expert_w_gate, expert_w_up, expert_w_down))
    mixer_tile = lambda i: jnp.minimum(i, n // tm - 1)
    halo_before = lambda halo: (lambda i: jnp.maximum(mixer_tile(i) * (tm // halo) - 1, 0))
    halo_after = lambda halo: (lambda i: jnp.minimum((mixer_tile(i) + 1) * (tm // halo), n // halo - 1))
    tile_spec = lambda col=0: pl.BlockSpec((tm, D_MODEL), lambda i: (mixer_tile(i), col))
    const2 = lambda i: (0, 0)
    for layer in range(DEPTH):
        kind, slot = layer % N_MIXERS, layer // N_MIXERS
        lng1, lnb1 = ln_g[layer, 0].reshape(1, D_MODEL), ln_b[layer, 0].reshape(1, D_MODEL)
        if kind == 0:
            qkv = _project(xb, attn_w_qkv[slot].astype(BF16), n_scaled=D_MODEL // MM_TN, scale=HEAD_DIM ** -0.5,
                           name="qkv_proj")
            o = _attention(qkv, _attn_bias(attn_rpb[slot]), attn_tables)
            args = (o, attn_w_o[slot].astype(BF16))
            specs = (tile_spec(), pl.BlockSpec((D_MODEL, D_MODEL), const2))
            post = _post_call("attn", n, args, specs, x, lng1, lnb1, route_ops, (), seqs)
        elif kind == 1:
            before, after = halo_before(POOL_HALO), halo_after(POOL_HALO)
            args = (x, x, x, pool_w[slot].astype(BF16), pool_scale[slot].reshape(1, D_MODEL))
            specs = (tile_spec(),
                     pl.BlockSpec((POOL_HALO, D_MODEL), lambda i: (before(i), 0)),
                     pl.BlockSpec((POOL_HALO, D_MODEL), lambda i: (after(i), 0)),
                     pl.BlockSpec((len(POOL_WINDOWS), POOL_GROUP_DIM, POOL_GROUP_DIM), lambda i: (0, 0, 0)),
                     pl.BlockSpec((1, D_MODEL), const2))
            scratch = (pltpu.VMEM((tm + 2 * POOL_HALO, D_MODEL), F32),)
            post = _post_call("pool", n, args, specs, x, lng1, lnb1, route_ops, scratch, seqs)
        else:
            bch = _project(xb, conv_w_in[slot].astype(BF16), name="conv_in_proj")
            before, after = halo_before(CONV_HALO), halo_after(CONV_HALO)
            prev = lambda c: pl.BlockSpec((CONV_HALO, D_MODEL), lambda i, c=c: (before(i), c))
            nxt = lambda c: pl.BlockSpec((CONV_HALO, D_MODEL), lambda i, c=c: (after(i), c))
            args = (bch,) * 7 + (conv_w[slot], conv_w_out[slot].astype(BF16))
            specs = (tile_spec(0), tile_spec(1), tile_spec(2),
                     prev(1), prev(2), nxt(1), nxt(2),
                     pl.BlockSpec((3, D_MODEL), const2),
                     pl.BlockSpec((D_MODEL, D_MODEL), const2))
            scratch = (pltpu.VMEM((tm + 2 * POOL_HALO, D_MODEL), F32),)
            post = _post_call("conv", n, args, specs, x, lng1, lnb1, route_ops, scratch, seqs)
        x1, ri, rg, cnt = post
        bexp, gsrc, sdst, n_blocks = _dispatch_tables(ri, cnt, n)
        y2 = _moe(x1, wg_all, wu_all, wd_all, layer, bexp, gsrc, sdst, n_blocks)
        gates = rg[0:2].T
        lng2, lnb2 = ln_g[layer, 1].reshape(1, D_MODEL), ln_b[layer, 1].reshape(1, D_MODEL)
        if layer + 1 < DEPTH:
            x, xb = _combine_ln(x1, y2, gates, lng2, lnb2)
    n0 = seqs[0][0] * seqs[0][1]
    (y_first,) = _combine_ln(x1, y2, gates, lng2, lnb2, first_row=0, n_rows=n0, with_bf16=False)
    (y_second,) = _combine_ln(x1, y2, gates, lng2, lnb2, first_row=n0, n_rows=n - n0, with_bf16=False)
    return y_first, y_second


def kernel(x_prompt, x_sample, attn_w_qkv, attn_w_o, attn_rpb, pool_w, pool_scale, conv_w_in, conv_w, conv_w_out,
           router_w, router_b, expert_w_gate, expert_w_up, expert_w_down, ln_g, ln_b):
    seqs = (x_prompt.shape[:2], x_sample.shape[:2])
    x = jnp.concatenate([x_prompt.reshape(-1, D_MODEL), x_sample.reshape(-1, D_MODEL)], axis=0)
    y_prompt, y_sample = _trunk(x, seqs, attn_w_qkv, attn_w_o, attn_rpb, pool_w, pool_scale, conv_w_in, conv_w,
                                conv_w_out, router_w, router_b, expert_w_gate, expert_w_up, expert_w_down, ln_g, ln_b)
    return y_prompt.reshape(x_prompt.shape), y_sample.reshape(x_sample.shape)
```

```python
import functools

import numpy as np
import jax
import jax.numpy as jnp
from jax import lax
from jax.experimental import pallas as pl
from jax.experimental.pallas import tpu as pltpu

F32 = jnp.float32
BF16 = jnp.bfloat16
I32 = jnp.int32
U32 = jnp.uint32

D_MODEL = 2048
DEPTH = 4
GRID_W = 64
N_MIXERS = 3
N_HEADS = 16
HEAD_DIM = D_MODEL // N_HEADS
WIN_ROWS = 8
WIN_COLS = 16
COL_BLOCK = 16
KEY_COLS = 2 * COL_BLOCK
N_COL_BLOCKS = GRID_W // COL_BLOCK
POOL_WINDOWS = (2, 4, 8, 16)
POOL_GROUP_DIM = D_MODEL // len(POOL_WINDOWS)
POOL_HALO = 8
CONV_HALO = 16
N_EXPERTS = 16
N_EXPERT_GROUPS = 4
EXPERTS_PER_GROUP = N_EXPERTS // N_EXPERT_GROUPS
D_EXPERT = D_MODEL // 2
ALPHA = (2 * DEPTH) ** 0.25
LN_EPS = 1e-5
MASK_VALUE = -1e30

MM_TM = 1024
MM_TN = 2048
ATT_ROWS = 8
ATT_TOK = ATT_ROWS * GRID_W
ATT_KROWS = 16
ATT_KSUB_ROWS = 4
ATT_KSUB_TOK = ATT_KSUB_ROWS * GRID_W
ATT_N_KSUB = ATT_KROWS // ATT_KSUB_ROWS
ATT_HEADS_PER_STEP = 4
POST_TM = 256
MOE_TM = 256
F32_SUBLANES = 8
MOE_LEAD_BLOCKS = 2
LN2_TM = 512
VMEM_LIMIT = 56 * 1024 * 1024


def _cparams(sem):
    return pltpu.CompilerParams(dimension_semantics=sem, vmem_limit_bytes=VMEM_LIMIT)


def _mm_kernel(x_ref, w_ref, o_ref, *, n_scaled, scale):
    acc = jnp.dot(x_ref[...], w_ref[...], preferred_element_type=F32)
    if n_scaled:
        acc = acc * jnp.where(pl.program_id(1) < n_scaled, F32(scale), F32(1.0))
    o_ref[...] = acc.astype(o_ref.dtype)


def _project(x, w, *, n_scaled=0, scale=1.0, name):
    m, k = x.shape
    n = w.shape[1]
    assert m % MM_TM == 0 and n % MM_TN == 0
    return pl.pallas_call(
        functools.partial(_mm_kernel, n_scaled=n_scaled, scale=scale),
        out_shape=jax.ShapeDtypeStruct((m, n), BF16),
        grid=(m // MM_TM, n // MM_TN),
        in_specs=[pl.BlockSpec((MM_TM, k), lambda i, j: (i, 0)),
                  pl.BlockSpec((k, MM_TN), lambda i, j: (0, j))],
        out_specs=pl.BlockSpec((MM_TM, MM_TN), lambda i, j: (i, j)),
        compiler_params=_cparams(("parallel", "arbitrary")),
        name=name,
    )(x, w)


def _key_col_start(j):
    return int(np.clip(j * COL_BLOCK - WIN_COLS // 2, 0, GRID_W - KEY_COLS))


def _attn_tables(seqs):
    blocks = []
    row0 = 0
    for batch, length in seqs:
        rows = length // GRID_W
        assert rows % ATT_ROWS == 0 and rows >= ATT_KROWS
        for _ in range(batch):
            for r0 in range(0, rows, ATT_ROWS):
                variant = 0 if r0 == 0 else (2 if r0 == rows - ATT_ROWS else 1)
                ks = row0 + int(np.clip(r0 - WIN_ROWS // 2, 0, rows - ATT_KROWS))
                blocks.append((variant, (row0 + r0) // ATT_ROWS, ks // ATT_KSUB_ROWS))
            row0 += rows
    blocks.sort(key=lambda b: b[0])
    var = np.array([b[0] for b in blocks], np.int32)
    gtab = np.array([b[1] for b in blocks], np.int32)
    kwin = np.array([b[2] for b in blocks], np.int32)
    return gtab, kwin, var


def _attn_bias(rpb):
    i = np.arange(ATT_ROWS)[:, None, None, None]
    cq = np.arange(COL_BLOCK)[None, :, None, None]
    rk = np.arange(ATT_KROWS)[None, None, :, None]
    ck = np.arange(KEY_COLS)[None, None, None, :]
    n_dy, n_dx = 2 * WIN_ROWS - 1, 2 * WIN_COLS - 1
    sel_dy = np.zeros((3, n_dy, ATT_ROWS, ATT_KROWS), np.float32)
    row_ok = np.zeros((3, ATT_ROWS, ATT_KROWS), bool)
    for variant in range(3):
        if variant == 0:
            rel_start = np.maximum(i - WIN_ROWS // 2, 0)
            dy = rk - i + WIN_ROWS - 1
        elif variant == 1:
            rel_start = i
            dy = rk - i + WIN_ROWS - 1 - WIN_ROWS // 2
        else:
            rel_start = np.minimum(i + WIN_ROWS // 2, ATT_KROWS - WIN_ROWS)
            dy = rk - i + WIN_ROWS - 1 - (ATT_KROWS - ATT_ROWS)
        row_ok[variant] = ((rk >= rel_start) & (rk < rel_start + WIN_ROWS))[:, 0, :, 0]
        dyc = np.clip(dy, 0, n_dy - 1)[:, 0, :, 0]
        sel_dy[variant] = np.arange(n_dy)[:, None, None] == dyc[None]
    sel_dx = np.zeros((N_COL_BLOCKS, n_dx, COL_BLOCK, KEY_COLS), np.float32)
    col_ok = np.zeros((N_COL_BLOCKS, COL_BLOCK, KEY_COLS), bool)
    for j in range(N_COL_BLOCKS):
        qc = j * COL_BLOCK + cq
        kc = _key_col_start(j) + ck
        c_start = np.clip(qc - WIN_COLS // 2, 0, GRID_W - WIN_COLS)
        col_ok[j] = ((kc >= c_start) & (kc < c_start + WIN_COLS))[0, :, 0, :]
        dxc = np.clip(kc - qc + WIN_COLS - 1, 0, n_dx - 1)[0, :, 0, :]
        sel_dx[j] = np.arange(n_dx)[:, None, None] == dxc[None]
    hp = lax.Precision.HIGHEST
    by = jnp.einsum("hab,vair->vhirb", rpb.astype(F32), jnp.asarray(sel_dy), precision=hp)
    bias = jnp.einsum("vhirb,jbck->vhjicrk", by, jnp.asarray(sel_dx), precision=hp)
    ok = row_ok[:, None, None, :, None, :, None] & col_ok[None, None, :, None, :, None, :]
    bias = jnp.where(jnp.asarray(ok), bias, F32(MASK_VALUE))
    return bias.reshape(3, N_HEADS, N_COL_BLOCKS, ATT_ROWS * COL_BLOCK, ATT_KROWS * KEY_COLS)


def _attn_kernel(gtab, kwin, var, q_ref, k0, k1, k2, k3, v0, v1, v2, v3, bias_ref, o_ref, kf_ref, vf_ref):
    del gtab, kwin, var
    krefs = (k0, k1, k2, k3)
    vrefs = (v0, v1, v2, v3)
    nt = (((1,), (1,)), ((), ()))
    lanes = [slice(hh * HEAD_DIM, (hh + 1) * HEAD_DIM) for hh in range(ATT_HEADS_PER_STEP)]
    work = [(hh, j) for hh in range(ATT_HEADS_PER_STEP) for j in range(N_COL_BLOCKS)]

    def window(ref, hh, j):
        kc0 = _key_col_start(j)
        return jnp.concatenate(
            [ref[hh, rk * GRID_W + kc0:rk * GRID_W + kc0 + KEY_COLS, :] for rk in range(ATT_KROWS)],
            axis=0).astype(BF16)

    for hh in range(ATT_HEADS_PER_STEP):
        for i in range(ATT_N_KSUB):
            kf_ref[hh, i * ATT_KSUB_TOK:(i + 1) * ATT_KSUB_TOK, :] = krefs[i][:, lanes[hh]].astype(F32)
            vf_ref[hh, i * ATT_KSUB_TOK:(i + 1) * ATT_KSUB_TOK, :] = vrefs[i][:, lanes[hh]].astype(F32)
    scores = []
    for hh, j in work:
        qj = jnp.concatenate(
            [q_ref[rr * GRID_W + j * COL_BLOCK:rr * GRID_W + (j + 1) * COL_BLOCK, lanes[hh]]
             for rr in range(ATT_ROWS)], axis=0)
        scores.append(lax.dot_general(qj, window(kf_ref, hh, j), nt, preferred_element_type=F32) + bias_ref[hh, j])
    probs = []
    for s in scores:
        m = jnp.max(s, axis=-1, keepdims=True)
        p = jnp.exp(s - m)
        probs.append((p.astype(BF16), jnp.sum(p, axis=-1, keepdims=True)))
    for (hh, j), (p, denom) in zip(work, probs):
        o = jnp.dot(p, window(vf_ref, hh, j), preferred_element_type=F32) / denom
        ob = o.astype(BF16)
        for rr in range(ATT_ROWS):
            o_ref[rr * GRID_W + j * COL_BLOCK:rr * GRID_W + (j + 1) * COL_BLOCK, lanes[hh]] = \
                ob[rr * COL_BLOCK:(rr + 1) * COL_BLOCK, :]


def _attention(qkv, bias, tables):
    n = qkv.shape[0]
    gtab, kwin, var = tables
    hw = ATT_HEADS_PER_STEP * HEAD_DIM
    n_hsteps = N_HEADS // ATT_HEADS_PER_STEP
    kv_specs = []
    for part in (1, 2):
        for i in range(ATT_N_KSUB):
            kv_specs.append(pl.BlockSpec(
                (ATT_KSUB_TOK, hw),
                lambda hp, s, gtab, kwin, var, i=i, part=part: (kwin[s] + i, part * n_hsteps + hp)))
    grid_spec = pltpu.PrefetchScalarGridSpec(
        num_scalar_prefetch=3,
        grid=(n_hsteps, len(gtab)),
        in_specs=[pl.BlockSpec((ATT_TOK, hw), lambda hp, s, gtab, kwin, var: (gtab[s], hp))] + kv_specs + [
            pl.BlockSpec((None, ATT_HEADS_PER_STEP, N_COL_BLOCKS, ATT_ROWS * COL_BLOCK, ATT_KROWS * KEY_COLS),
                         lambda hp, s, gtab, kwin, var: (var[s], hp, 0, 0, 0))],
        out_specs=pl.BlockSpec((ATT_TOK, hw), lambda hp, s, gtab, kwin, var: (gtab[s], hp)),
        scratch_shapes=[pltpu.VMEM((ATT_HEADS_PER_STEP, ATT_KROWS * GRID_W, HEAD_DIM), F32),
                        pltpu.VMEM((ATT_HEADS_PER_STEP, ATT_KROWS * GRID_W, HEAD_DIM), F32)],
    )
    return pl.pallas_call(
        _attn_kernel,
        out_shape=jax.ShapeDtypeStruct((n, D_MODEL), BF16),
        grid_spec=grid_spec,
        compiler_params=_cparams(("parallel", "arbitrary")),
        name="nbr_attention",
    )(jnp.asarray(gtab), jnp.asarray(kwin), jnp.asarray(var), qkv, *([qkv] * (2 * ATT_N_KSUB)), bias)


def _first_max4(vals):
    best = vals[0]
    idx = jnp.zeros(best.shape, I32)
    for i in range(1, 4):
        take = vals[i] > best
        best = jnp.where(take, vals[i], best)
        idx = jnp.where(take, I32(i), idx)
    return idx


def _select4(idx, vals):
    return jnp.where(idx == 0, vals[0], jnp.where(idx == 1, vals[1], jnp.where(idx == 2, vals[2], vals[3])))


def _ln_route(z, counted, lng_ref, lnb_ref, rw2_ref, rwhi_ref, rb_ref, tri_ref, x1_ref, ri_ref, rg_ref, cnt_ref,
              carry_ref):
    tm = z.shape[0]

    mu = jnp.mean(z, axis=-1, keepdims=True)
    zc = z - mu
    var = jnp.mean(zc * zc, axis=-1, keepdims=True)
    x1 = zc * lax.rsqrt(var + LN_EPS) * lng_ref[...] + lnb_ref[...]
    x1_ref[...] = x1

    hi = x1.astype(BF16)
    lo = (x1 - hi.astype(F32)).astype(BF16)
    nt = (((1,), (1,)), ((), ()))
    l2 = lax.dot_general(rw2_ref[...], hi, nt, preferred_element_type=F32)
    l1 = lax.dot_general(rwhi_ref[...], lo, nt, preferred_element_type=F32)
    logits = l2[:N_EXPERTS] + l2[N_EXPERTS:] + l1
    ex = jnp.exp(logits - jnp.max(logits, axis=0, keepdims=True))
    scores = ex / jnp.sum(ex, axis=0, keepdims=True)
    biased = scores + rb_ref[...]
    sr = [scores[e:e + 1, :] for e in range(N_EXPERTS)]
    br = [biased[e:e + 1, :] for e in range(N_EXPERTS)]

    group_scores = []
    for g in range(N_EXPERT_GROUPS):
        a, b, c, d = br[g * EXPERTS_PER_GROUP:(g + 1) * EXPERTS_PER_GROUP]
        hi1, lo1, hi2, lo2 = jnp.maximum(a, b), jnp.minimum(a, b), jnp.maximum(c, d), jnp.minimum(c, d)
        group_scores.append(jnp.maximum(hi1, hi2) + jnp.maximum(jnp.minimum(hi1, hi2), jnp.maximum(lo1, lo2)))
    gsel = _first_max4(group_scores)
    vb = [_select4(gsel, [br[g * EXPERTS_PER_GROUP + i] for g in range(N_EXPERT_GROUPS)]) for i in range(4)]
    vs = [_select4(gsel, [sr[g * EXPERTS_PER_GROUP + i] for g in range(N_EXPERT_GROUPS)]) for i in range(4)]
    i1 = _first_max4(vb)
    i2 = _first_max4([jnp.where(i1 == i, -jnp.inf, vb[i]) for i in range(4)])
    e0 = gsel * EXPERTS_PER_GROUP + i1
    e1 = gsel * EXPERTS_PER_GROUP + i2
    w0 = _select4(i1, vs)
    w1 = _select4(i2, vs)
    den = w0 + w1

    eio = lax.broadcasted_iota(I32, (N_EXPERTS, tm), 0)
    oh0 = eio == e0
    oh1 = eio == e1
    ohs = jnp.where(oh0 | oh1, F32(1.0), F32(0.0))
    carry = carry_ref[:, 0:1]
    before = jnp.dot(ohs.astype(BF16), tri_ref[...], preferred_element_type=F32) + carry
    r0 = jnp.sum(jnp.where(oh0, before, 0.0), axis=0, keepdims=True)
    r1 = jnp.sum(jnp.where(oh1, before, 0.0), axis=0, keepdims=True)
    tile_counts = jnp.where(counted, jnp.sum(ohs, axis=1, keepdims=True), 0.0)
    carry_ref[...] = jnp.broadcast_to(carry + tile_counts, carry_ref.shape)
    cnt_ref[...] = carry_ref[...]
    ri_ref[...] = jnp.concatenate([e0, e1, r0.astype(I32), r1.astype(I32), jnp.zeros((4, tm), I32)], axis=0)
    rg_ref[...] = jnp.concatenate([w0 / den, w1 / den, jnp.zeros((6, tm), F32)], axis=0)


def _seq_position(tile, tm, seqs):
    (b0, l0), (_, l1) = seqs
    start = tile * tm
    in_first = start < b0 * l0
    length = jnp.where(in_first, I32(l0), I32(l1))
    pos0 = jnp.where(in_first, start % l0, (start - b0 * l0) % l1)
    return pos0 + lax.broadcasted_iota(I32, (tm, 1), 0), length


def _mixer_tile():
    return jnp.minimum(pl.program_id(0), pl.num_programs(0) - 2)


def _pipelined_post(mix_fn, x_ref, route, y_ref, carry_ref):
    i = pl.program_id(0)

    @pl.when(i == 0)
    def _():
        y_ref[...] = jnp.zeros_like(y_ref)
        carry_ref[...] = jnp.zeros_like(carry_ref)

    z = ALPHA * x_ref[...] + y_ref[...]
    y_ref[...] = mix_fn()
    _ln_route(z, i > 0, *route, carry_ref)


def _post_attn_kernel(o_ref, w_ref, x_ref, *rest):
    *route, carry_ref, y_ref = rest
    _pipelined_post(lambda: jnp.dot(o_ref[...], w_ref[...], preferred_element_type=F32), x_ref, route, y_ref,
                    carry_ref)


def _post_conv_kernel(b_ref, c_ref, h_ref, cp_ref, hp_ref, cn_ref, hn_ref, cw_ref, w_ref, x_ref, *rest, seqs):
    *route, carry_ref, y_ref, u_ref = rest
    _pipelined_post(functools.partial(_conv_mix, b_ref, c_ref, h_ref, cp_ref, hp_ref, cn_ref, hn_ref, cw_ref, w_ref,
                                      u_ref, seqs), x_ref, route, y_ref, carry_ref)


def _conv_mix(b_ref, c_ref, h_ref, cp_ref, hp_ref, cn_ref, hn_ref, cw_ref, w_ref, u_ref, seqs):
    tm = b_ref.shape[0]
    pos, length = _seq_position(_mixer_tile(), tm, seqs)
    h8 = POOL_HALO
    u_before = (cp_ref[...].astype(F32) * hp_ref[...].astype(F32))[CONV_HALO - h8:, :]
    u_after = (cn_ref[...].astype(F32) * hn_ref[...].astype(F32))[0:h8, :]
    u_ref[0:h8, :] = jnp.where(pos[0:1, :] > 0, u_before, 0.0)
    u_ref[h8:h8 + tm, :] = c_ref[...].astype(F32) * h_ref[...].astype(F32)
    u_ref[h8 + tm:, :] = jnp.where(pos[tm - 1:tm, :] + 1 < length, u_after, 0.0)
    prev = u_ref[h8 - 1:h8 - 1 + tm, :]
    nxt = u_ref[h8 + 1:h8 + 1 + tm, :]
    conv = prev * cw_ref[0:1, :] + u_ref[h8:h8 + tm, :] * cw_ref[1:2, :] + nxt * cw_ref[2:3, :]
    zin = (b_ref[...].astype(F32) * conv).astype(BF16)
    return jnp.dot(zin, w_ref[...], preferred_element_type=F32)


def _post_pool_kernel(xm_ref, xp_ref, xn_ref, w_ref, sc_ref, x_ref, *rest, seqs):
    *route, carry_ref, y_ref, xs_ref = rest
    _pipelined_post(functools.partial(_pool_mix, xm_ref, xp_ref, xn_ref, w_ref, sc_ref, xs_ref, seqs), x_ref, route,
                    y_ref, carry_ref)


def _pool_mix(xm_ref, xp_ref, xn_ref, w_ref, sc_ref, xs_ref, seqs):
    tm = xm_ref.shape[0]
    pos, length = _seq_position(_mixer_tile(), tm, seqs)
    h8 = POOL_HALO
    xs_ref[0:h8, :] = jnp.where(pos[0:1, :] > 0, xp_ref[...], 0.0)
    xs_ref[h8:h8 + tm, :] = xm_ref[...]
    xs_ref[h8 + tm:, :] = jnp.where(pos[tm - 1:tm, :] + 1 < length, xn_ref[...], 0.0)
    ys = []
    for g, w in enumerate(POOL_WINDOWS):
        cs = slice(g * POOL_GROUP_DIM, (g + 1) * POOL_GROUP_DIM)
        acc = jnp.zeros((tm, POOL_GROUP_DIM), F32)
        for s in range(-(w // 2), w - w // 2):
            acc = acc + xs_ref[h8 + s:h8 + s + tm, cs]
        lo = jnp.maximum(pos - w // 2, 0)
        hi = jnp.minimum(pos - w // 2 + w, length)
        pooled = acc / (hi - lo).astype(F32) - xm_ref[:, cs]
        ys.append(jnp.dot(pooled.astype(BF16), w_ref[g], preferred_element_type=F32))
    return jnp.concatenate(ys, axis=1) * sc_ref[...]


def _route_operands(router_w, router_b, tm):
    rw_t = router_w.T
    rw_hi = rw_t.astype(BF16)
    rw_lo = (rw_t - rw_hi.astype(F32)).astype(BF16)
    tri = jnp.asarray(np.triu(np.ones((tm, tm), np.float32), 1), BF16)
    return jnp.concatenate([rw_hi, rw_lo], axis=0), rw_hi, router_b.reshape(N_EXPERTS, 1).astype(F32), tri


def _post_call(kind, n, mixer_args, mixer_specs, x, lng, lnb, route_ops, scratch, seqs):
    tm = POST_TM
    rw2, rwhi, rb, tri = route_ops
    const = lambda shape: pl.BlockSpec(shape, lambda i: (0,) * len(shape))
    done = lambda i: jnp.maximum(i - 1, 0)
    in_specs = list(mixer_specs) + [
        pl.BlockSpec((tm, D_MODEL), lambda i: (done(i), 0)),
        const((1, D_MODEL)), const((1, D_MODEL)),
        const((2 * N_EXPERTS, D_MODEL)), const((N_EXPERTS, D_MODEL)), const((N_EXPERTS, 1)), const((tm, tm))]
    out_shape = [jax.ShapeDtypeStruct((n, D_MODEL), F32),
                 jax.ShapeDtypeStruct((8, n), I32),
                 jax.ShapeDtypeStruct((8, n), F32),
                 jax.ShapeDtypeStruct((N_EXPERTS, 128), F32)]
    out_specs = [pl.BlockSpec((tm, D_MODEL), lambda i: (done(i), 0)),
                 pl.BlockSpec((8, tm), lambda i: (0, done(i))),
                 pl.BlockSpec((8, tm), lambda i: (0, done(i))),
                 const((N_EXPERTS, 128))]
    body = {"attn": _post_attn_kernel,
            "conv": functools.partial(_post_conv_kernel, seqs=seqs),
            "pool": functools.partial(_post_pool_kernel, seqs=seqs)}[kind]
    return pl.pallas_call(
        body,
        out_shape=out_shape,
        grid=(n // tm + 1,),
        in_specs=in_specs,
        out_specs=out_specs,
        scratch_shapes=[pltpu.VMEM((N_EXPERTS, 128), F32), pltpu.VMEM((tm, D_MODEL), F32)] + list(scratch),
        compiler_params=_cparams(("arbitrary",)),
        name="post_" + kind,
    )(*mixer_args, x, lng, lnb, rw2, rwhi, rb, tri)


def _pack_bf16_pairs(y):
    half = y.shape[1] // 2
    lo = pltpu.bitcast(y[:, :half].astype(BF16).astype(F32), U32)
    hi = pltpu.bitcast(y[:, half:].astype(BF16).astype(F32), U32)
    return (lo >> 16) | (hi & jnp.uint32(0xFFFF0000))


def _unpack_bf16_pairs(p):
    lo = pltpu.bitcast(p << 16, F32)
    hi = pltpu.bitcast(p & jnp.uint32(0xFFFF0000), F32)
    return jnp.concatenate([lo, hi], axis=1)


def _moe_kernel(bexp, gsrc_ref, sdst_ref, x_hbm, wg_ref, wu_ref, wd_ref, out_hbm, xbuf, ybuf, gsem, ssem):
    del bexp
    groups, sub = ybuf.shape[1], ybuf.shape[2]
    tm = groups * sub
    s = pl.program_id(0)
    last = pl.num_programs(0) - 1
    cur = s % 2
    prev = 1 - cur

    def gather_row(g, u):
        tok = gsrc_ref[0, g * sub + u]
        return pltpu.make_async_copy(x_hbm.at[pl.ds(tok, 1)], xbuf.at[cur, g, pl.ds(u, 1)], gsem.at[cur])

    def scatter_row(g, u):
        row = sdst_ref[0, g * sub + u]
        return pltpu.make_async_copy(ybuf.at[cur, g, pl.ds(u, 1)], out_hbm.at[pl.ds(row, 1)], ssem.at[cur])

    def gather_wait(sl):
        pltpu.make_async_copy(xbuf.at[sl], xbuf.at[sl], gsem.at[sl]).wait()

    def scatter_wait(sl):
        pltpu.make_async_copy(ybuf.at[sl], ybuf.at[sl], ssem.at[sl]).wait()

    @pl.when(s == 0)
    def _():
        ybuf[...] = jnp.zeros_like(ybuf)

    @pl.when(s > 0)
    def _():
        gather_wait(prev)
        scatter_wait(prev)

    def start_rows(g, carry):
        for u in range(sub):
            gather_row(g, u).start()
            scatter_row(g, u).start()
        return carry
    lax.fori_loop(0, groups, start_rows, 0)

    @pl.when((s > 0) & (s < last))
    def _():
        x = xbuf[prev].reshape(tm, D_MODEL).astype(BF16)
        gate = jnp.dot(x, wg_ref[...], preferred_element_type=F32)
        up = jnp.dot(x, wu_ref[...], preferred_element_type=F32)
        hdn = (gate * jax.nn.sigmoid(gate) * up).astype(BF16)
        y = jnp.dot(hdn, wd_ref[...], preferred_element_type=F32)
        ybuf[prev] = _pack_bf16_pairs(y).reshape(groups, sub, D_MODEL // 2)

    @pl.when(s == last)
    def _():
        gather_wait(cur)
        scatter_wait(cur)


def _moe(x1, wg, wu, wd, layer, bexp, gsrc, sdst, n_blocks):
    n = x1.shape[0]
    tm = MOE_TM
    block_of_step = lambda s: jnp.clip(s - 1, 0, n_blocks - 1)
    wspec = lambda shape: pl.BlockSpec((None, None) + shape, lambda s, bexp: (layer, bexp[block_of_step(s)], 0, 0))
    smem_row = lambda lead: pl.BlockSpec((None, 1, tm), lambda s, bexp: (s + lead, 0, 0), memory_space=pltpu.SMEM)
    grid_spec = pltpu.PrefetchScalarGridSpec(
        num_scalar_prefetch=1,
        grid=(n_blocks + 2,),
        in_specs=[smem_row(MOE_LEAD_BLOCKS), smem_row(MOE_LEAD_BLOCKS - 2),
                  pl.BlockSpec(memory_space=pl.ANY),
                  wspec((D_MODEL, D_EXPERT)), wspec((D_MODEL, D_EXPERT)), wspec((D_EXPERT, D_MODEL))],
        out_specs=pl.BlockSpec(memory_space=pl.ANY),
        scratch_shapes=[pltpu.VMEM((2, tm // F32_SUBLANES, F32_SUBLANES, D_MODEL), F32),
                        pltpu.VMEM((2, tm // F32_SUBLANES, F32_SUBLANES, D_MODEL // 2), U32),
                        pltpu.SemaphoreType.DMA((2,)), pltpu.SemaphoreType.DMA((2,))],
    )
    return pl.pallas_call(
        _moe_kernel,
        out_shape=jax.ShapeDtypeStruct((2 * n + tm, D_MODEL // 2), U32),
        grid_spec=grid_spec,
        compiler_params=_cparams(("arbitrary",)),
        name="moe_experts",
    )(bexp, gsrc, sdst, x1, wg, wu, wd)


def _dispatch_tables(ri, cnt, n):
    tm = MOE_TM
    n_blocks = -(-2 * n // tm) + N_EXPERTS
    n_table_blocks = n_blocks + 2 * MOE_LEAD_BLOCKS
    experts = ri[0:2]
    rank = ri[2:4]
    counts = cnt[:, 0].astype(I32)
    nblk = (counts + tm - 1) // tm
    bend = jnp.cumsum(nblk)
    bstart = bend - nblk
    first_block = jnp.sum(jnp.where(experts[:, :, None] == jnp.arange(N_EXPERTS, dtype=I32), bstart, 0), axis=-1)
    dest = (first_block + MOE_LEAD_BLOCKS) * tm + rank
    code = 2 * jnp.arange(n, dtype=I32)[None, :] + jnp.arange(2, dtype=I32)[:, None]
    rows = jnp.arange(n_table_blocks * tm, dtype=I32)
    inv = jnp.full(rows.shape, -1, I32).at[dest.reshape(-1)].set(code.reshape(-1), unique_indices=True)
    gsrc = jnp.where(inv < 0, n - 1, inv >> 1)
    sdst = jnp.where(inv < 0, 2 * n + rows % tm, (inv & 1) * n + (inv >> 1))
    blocks = jnp.arange(n_blocks, dtype=I32)
    bexp_raw = jnp.minimum(jnp.sum(blocks[:, None] >= bend[None, :], axis=1), N_EXPERTS - 1).astype(I32)
    bexp = bexp_raw[jnp.minimum(blocks, bend[-1] - 1)]
    shape3 = (n_table_blocks, 1, tm)
    return bexp, gsrc.reshape(shape3), sdst.reshape(shape3), n_blocks


def _ln2_kernel(x_ref, y0_ref, y1_ref, g_ref, lng_ref, lnb_ref, o_ref, *maybe_ob_ref):
    z = (ALPHA * x_ref[...] + _unpack_bf16_pairs(y0_ref[...]) * g_ref[:, 0:1]
         + _unpack_bf16_pairs(y1_ref[...]) * g_ref[:, 1:2])
    mu = jnp.mean(z, axis=-1, keepdims=True)
    zc = z - mu
    var = jnp.mean(zc * zc, axis=-1, keepdims=True)
    out = zc * lax.rsqrt(var + LN_EPS) * lng_ref[...] + lnb_ref[...]
    o_ref[...] = out
    for ob_ref in maybe_ob_ref:
        ob_ref[...] = out.astype(BF16)


def _combine_ln(x1, y2, gates, lng, lnb, *, first_row=0, n_rows=None, with_bf16=True):
    tm = LN2_TM
    n_rows = x1.shape[0] if n_rows is None else n_rows
    assert first_row % tm == 0 and n_rows % tm == 0
    off = first_row // tm
    slot1 = x1.shape[0] // tm
    row = lambda i: (i + off, 0)
    const = lambda i: (0, 0)
    out_shape = [jax.ShapeDtypeStruct((n_rows, D_MODEL), F32)]
    out_specs = [pl.BlockSpec((tm, D_MODEL), lambda i: (i, 0))]
    if with_bf16:
        out_shape.append(jax.ShapeDtypeStruct((n_rows, D_MODEL), BF16))
        out_specs.append(pl.BlockSpec((tm, D_MODEL), lambda i: (i, 0)))
    return pl.pallas_call(
        _ln2_kernel,
        out_shape=out_shape,
        grid=(n_rows // tm,),
        in_specs=[pl.BlockSpec((tm, D_MODEL), row),
                  pl.BlockSpec((tm, D_MODEL // 2), row),
                  pl.BlockSpec((tm, D_MODEL // 2), lambda i: (i + off + slot1, 0)),
                  pl.BlockSpec((tm, 2), row),
                  pl.BlockSpec((1, D_MODEL), const), pl.BlockSpec((1, D_MODEL), const)],
        out_specs=out_specs,
        compiler_params=_cparams(("parallel",)),
        name="combine_ln",
    )(x1, y2, y2, gates, lng, lnb)


def _trunk(x, seqs, attn_w_qkv, attn_w_o, attn_rpb, pool_w, pool_scale, conv_w_in, conv_w, conv_w_out,
           router_w, router_b, expert_w_gate, expert_w_up, expert_w_down, ln_g, ln_b):
    n = x.shape[0]
    tm = POST_TM
    assert n % MM_TM == 0 and n % LN2_TM == 0 and all(l % tm == 0 for _, l in seqs)
    route_ops = _route_operands(router_w, router_b, tm)
    attn_tables = _attn_tables(seqs)
    xb = x.astype(BF16)
    wg_all, wu_all, wd_all = (w.astype(BF16) for w in (expert_w_gate, expert_w_up, expert_w_down))
    mixer_tile = lambda i: jnp.minimum(i, n // tm - 1)
    halo_before = lambda halo: (lambda i: jnp.maximum(mixer_tile(i) * (tm // halo) - 1, 0))
    halo_after = lambda halo: (lambda i: jnp.minimum((mixer_tile(i) + 1) * (tm // halo), n // halo - 1))
    tile_spec = lambda col=0: pl.BlockSpec((tm, D_MODEL), lambda i: (mixer_tile(i), col))
    const2 = lambda i: (0, 0)
    for layer in range(DEPTH):
        kind, slot = layer % N_MIXERS, layer // N_MIXERS
        lng1, lnb1 = ln_g[layer, 0].reshape(1, D_MODEL), ln_b[layer, 0].reshape(1, D_MODEL)
        if kind == 0:
            qkv = _project(xb, attn_w_qkv[slot].astype(BF16), n_scaled=D_MODEL // MM_TN, scale=HEAD_DIM ** -0.5,
                           name="qkv_proj")
            o = _attention(qkv, _attn_bias(attn_rpb[slot]), attn_tables)
            args = (o, attn_w_o[slot].astype(BF16))
            specs = (tile_spec(), pl.BlockSpec((D_MODEL, D_MODEL), const2))
            post = _post_call("attn", n, args, specs, x, lng1, lnb1, route_ops, (), seqs)
        elif kind == 1:
            before, after = halo_before(POOL_HALO), halo_after(POOL_HALO)
            args = (x, x, x, pool_w[slot].astype(BF16), pool_scale[slot].reshape(1, D_MODEL))
            specs = (tile_spec(),
                     pl.BlockSpec((POOL_HALO, D_MODEL), lambda i: (before(i), 0)),
                     pl.BlockSpec((POOL_HALO, D_MODEL), lambda i: (after(i), 0)),
                     pl.BlockSpec((len(POOL_WINDOWS), POOL_GROUP_DIM, POOL_GROUP_DIM), lambda i: (0, 0, 0)),
                     pl.BlockSpec((1, D_MODEL), const2))
            scratch = (pltpu.VMEM((tm + 2 * POOL_HALO, D_MODEL), F32),)
            post = _post_call("pool", n, args, specs, x, lng1, lnb1, route_ops, scratch, seqs)
        else:
            bch = _project(xb, conv_w_in[slot].astype(BF16), name="conv_in_proj")
            before, after = halo_before(CONV_HALO), halo_after(CONV_HALO)
            prev = lambda c: pl.BlockSpec((CONV_HALO, D_MODEL), lambda i, c=c: (before(i), c))
            nxt = lambda c: pl.BlockSpec((CONV_HALO, D_MODEL), lambda i, c=c: (after(i), c))
            args = (bch,) * 7 + (conv_w[slot], conv_w_out[slot].astype(BF16))
            specs = (tile_spec(0), tile_spec(1), tile_spec(2),
                     prev(1), prev(2), nxt(1), nxt(2),
                     pl.BlockSpec((3, D_MODEL), const2),
                     pl.BlockSpec((D_MODEL, D_MODEL), const2))
            scratch = (pltpu.VMEM((tm + 2 * POOL_HALO, D_MODEL), F32),)
            post = _post_call("conv", n, args, specs, x, lng1, lnb1, route_ops, scratch, seqs)
        x1, ri, rg, cnt = post
        bexp, gsrc, sdst, n_blocks = _dispatch_tables(ri, cnt, n)
        y2 = _moe(x1, wg_all, wu_all, wd_all, layer, bexp, gsrc, sdst, n_blocks)
        gates = rg[0:2].T
        lng2, lnb2 = ln_g[layer, 1].reshape(1, D_MODEL), ln_b[layer, 1].reshape(1, D_MODEL)
        if layer + 1 < DEPTH:
            x, xb = _combine_ln(x1, y2, gates, lng2, lnb2)
    n0 = seqs[0][0] * seqs[0][1]
    (y_first,) = _combine_ln(x1, y2, gates, lng2, lnb2, first_row=0, n_rows=n0, with_bf16=False)
    (y_second,) = _combine_ln(x1, y2, gates, lng2, lnb2, first_row=n0, n_rows=n - n0, with_bf16=False)
    return y_first, y_second


def kernel(x_prompt, x_sample, attn_w_qkv, attn_w_o, attn_rpb, pool_w, pool_scale, conv_w_in, conv_w, conv_w_out,
           router_w, router_b, expert_w_gate, expert_w_up, expert_w_down, ln_g, ln_b):
    seqs = (x_prompt.shape[:2], x_sample.shape[:2])
    x = jnp.concatenate([x_prompt.reshape(-1, D_MODEL), x_sample.reshape(-1, D_MODEL)], axis=0)
    y_prompt, y_sample = _trunk(x, seqs, attn_w_qkv, attn_w_o, attn_rpb, pool_w, pool_scale, conv_w_in, conv_w,
                                conv_w_out, router_w, router_b, expert_w_gate, expert_w_up, expert_w_down, ln_g, ln_b)
    return y_prompt.reshape(x_prompt.shape), y_sample.reshape(x_sample.shape)
```

```python
import functools

import numpy as np
import jax
import jax.numpy as jnp
from jax import lax
from jax.experimental import pallas as pl
from jax.experimental.pallas import tpu as pltpu

F32 = jnp.float32
BF16 = jnp.bfloat16
I32 = jnp.int32
U32 = jnp.uint32

D_MODEL = 2048
DEPTH = 4
GRID_W = 64
N_MIXERS = 3
N_HEADS = 16
HEAD_DIM = D_MODEL // N_HEADS
WIN_ROWS = 8
WIN_COLS = 16
COL_BLOCK = 16
KEY_COLS = 2 * COL_BLOCK
N_COL_BLOCKS = GRID_W // COL_BLOCK
POOL_WINDOWS = (2, 4, 8, 16)
POOL_GROUP_DIM = D_MODEL // len(POOL_WINDOWS)
POOL_HALO = 8
CONV_HALO = 16
N_EXPERTS = 16
N_EXPERT_GROUPS = 4
EXPERTS_PER_GROUP = N_EXPERTS // N_EXPERT_GROUPS
D_EXPERT = D_MODEL // 2
ALPHA = (2 * DEPTH) ** 0.25
LN_EPS = 1e-5
MASK_VALUE = -1e30

MM_TM = 1024
MM_TN = 2048
ATT_ROWS = 8
ATT_TOK = ATT_ROWS * GRID_W
ATT_KROWS = 16
ATT_KSUB_ROWS = 4
ATT_KSUB_TOK = ATT_KSUB_ROWS * GRID_W
ATT_N_KSUB = ATT_KROWS // ATT_KSUB_ROWS
ATT_HEADS_PER_STEP = 4
POST_TM = 256
MOE_TM = 256
F32_SUBLANES = 8
TABLE_UNROLL = 8
MOE_LEAD_BLOCKS = 2
LN2_TM = 512
VMEM_LIMIT = 56 * 1024 * 1024


def _cparams(sem):
    return pltpu.CompilerParams(dimension_semantics=sem, vmem_limit_bytes=VMEM_LIMIT)


def _mm_kernel(x_ref, w_ref, o_ref, *, n_scaled, scale):
    acc = jnp.dot(x_ref[...], w_ref[...], preferred_element_type=F32)
    if n_scaled:
        acc = acc * jnp.where(pl.program_id(1) < n_scaled, F32(scale), F32(1.0))
    o_ref[...] = acc.astype(o_ref.dtype)


def _project(x, w, *, n_scaled=0, scale=1.0, name):
    m, k = x.shape
    n = w.shape[1]
    assert m % MM_TM == 0 and n % MM_TN == 0
    return pl.pallas_call(
        functools.partial(_mm_kernel, n_scaled=n_scaled, scale=scale),
        out_shape=jax.ShapeDtypeStruct((m, n), BF16),
        grid=(m // MM_TM, n // MM_TN),
        in_specs=[pl.BlockSpec((MM_TM, k), lambda i, j: (i, 0)),
                  pl.BlockSpec((k, MM_TN), lambda i, j: (0, j))],
        out_specs=pl.BlockSpec((MM_TM, MM_TN), lambda i, j: (i, j)),
        compiler_params=_cparams(("parallel", "arbitrary")),
        name=name,
    )(x, w)


def _key_col_start(j):
    return int(np.clip(j * COL_BLOCK - WIN_COLS // 2, 0, GRID_W - KEY_COLS))


def _attn_tables(seqs):
    blocks = []
    row0 = 0
    for batch, length in seqs:
        rows = length // GRID_W
        assert rows % ATT_ROWS == 0 and rows >= ATT_KROWS
        for _ in range(batch):
            for r0 in range(0, rows, ATT_ROWS):
                variant = 0 if r0 == 0 else (2 if r0 == rows - ATT_ROWS else 1)
                ks = row0 + int(np.clip(r0 - WIN_ROWS // 2, 0, rows - ATT_KROWS))
                blocks.append((variant, (row0 + r0) // ATT_ROWS, ks // ATT_KSUB_ROWS))
            row0 += rows
    blocks.sort(key=lambda b: b[0])
    var = np.array([b[0] for b in blocks], np.int32)
    gtab = np.array([b[1] for b in blocks], np.int32)
    kwin = np.array([b[2] for b in blocks], np.int32)
    return gtab, kwin, var


def _attn_bias(rpb):
    i = np.arange(ATT_ROWS)[:, None, None, None]
    cq = np.arange(COL_BLOCK)[None, :, None, None]
    rk = np.arange(ATT_KROWS)[None, None, :, None]
    ck = np.arange(KEY_COLS)[None, None, None, :]
    n_dy, n_dx = 2 * WIN_ROWS - 1, 2 * WIN_COLS - 1
    sel_dy = np.zeros((3, n_dy, ATT_ROWS, ATT_KROWS), np.float32)
    row_ok = np.zeros((3, ATT_ROWS, ATT_KROWS), bool)
    for variant in range(3):
        if variant == 0:
            rel_start = np.maximum(i - WIN_ROWS // 2, 0)
            dy = rk - i + WIN_ROWS - 1
        elif variant == 1:
            rel_start = i
            dy = rk - i + WIN_ROWS - 1 - WIN_ROWS // 2
        else:
            rel_start = np.minimum(i + WIN_ROWS // 2, ATT_KROWS - WIN_ROWS)
            dy = rk - i + WIN_ROWS - 1 - (ATT_KROWS - ATT_ROWS)
        row_ok[variant] = ((rk >= rel_start) & (rk < rel_start + WIN_ROWS))[:, 0, :, 0]
        dyc = np.clip(dy, 0, n_dy - 1)[:, 0, :, 0]
        sel_dy[variant] = np.arange(n_dy)[:, None, None] == dyc[None]
    sel_dx = np.zeros((N_COL_BLOCKS, n_dx, COL_BLOCK, KEY_COLS), np.float32)
    col_ok = np.zeros((N_COL_BLOCKS, COL_BLOCK, KEY_COLS), bool)
    for j in range(N_COL_BLOCKS):
        qc = j * COL_BLOCK + cq
        kc = _key_col_start(j) + ck
        c_start = np.clip(qc - WIN_COLS // 2, 0, GRID_W - WIN_COLS)
        col_ok[j] = ((kc >= c_start) & (kc < c_start + WIN_COLS))[0, :, 0, :]
        dxc = np.clip(kc - qc + WIN_COLS - 1, 0, n_dx - 1)[0, :, 0, :]
        sel_dx[j] = np.arange(n_dx)[:, None, None] == dxc[None]
    hp = lax.Precision.HIGHEST
    by = jnp.einsum("hab,vair->vhirb", rpb.astype(F32), jnp.asarray(sel_dy), precision=hp)
    bias = jnp.einsum("vhirb,jbck->vhjicrk", by, jnp.asarray(sel_dx), precision=hp)
    ok = row_ok[:, None, None, :, None, :, None] & col_ok[None, None, :, None, :, None, :]
    bias = jnp.where(jnp.asarray(ok), bias, F32(MASK_VALUE))
    return bias.reshape(3, N_HEADS, N_COL_BLOCKS, ATT_ROWS * COL_BLOCK, ATT_KROWS * KEY_COLS)


def _attn_kernel(gtab, kwin, var, q_ref, k0, k1, k2, k3, v0, v1, v2, v3, bias_ref, o_ref, kf_ref, vf_ref):
    del gtab, kwin, var
    krefs = (k0, k1, k2, k3)
    vrefs = (v0, v1, v2, v3)
    nt = (((1,), (1,)), ((), ()))
    lanes = [slice(hh * HEAD_DIM, (hh + 1) * HEAD_DIM) for hh in range(ATT_HEADS_PER_STEP)]
    work = [(hh, j) for hh in range(ATT_HEADS_PER_STEP) for j in range(N_COL_BLOCKS)]

    def window(ref, hh, j):
        kc0 = _key_col_start(j)
        return jnp.concatenate(
            [ref[hh, rk * GRID_W + kc0:rk * GRID_W + kc0 + KEY_COLS, :] for rk in range(ATT_KROWS)],
            axis=0).astype(BF16)

    for hh in range(ATT_HEADS_PER_STEP):
        for i in range(ATT_N_KSUB):
            kf_ref[hh, i * ATT_KSUB_TOK:(i + 1) * ATT_KSUB_TOK, :] = krefs[i][:, lanes[hh]].astype(F32)
            vf_ref[hh, i * ATT_KSUB_TOK:(i + 1) * ATT_KSUB_TOK, :] = vrefs[i][:, lanes[hh]].astype(F32)
    scores = []
    for hh, j in work:
        qj = jnp.concatenate(
            [q_ref[rr * GRID_W + j * COL_BLOCK:rr * GRID_W + (j + 1) * COL_BLOCK, lanes[hh]]
             for rr in range(ATT_ROWS)], axis=0)
        scores.append(lax.dot_general(qj, window(kf_ref, hh, j), nt, preferred_element_type=F32) + bias_ref[hh, j])
    probs = []
    for s in scores:
        m = jnp.max(s, axis=-1, keepdims=True)
        p = jnp.exp(s - m)
        probs.append((p.astype(BF16), jnp.sum(p, axis=-1, keepdims=True)))
    for (hh, j), (p, denom) in zip(work, probs):
        o = jnp.dot(p, window(vf_ref, hh, j), preferred_element_type=F32) / denom
        ob = o.astype(BF16)
        for rr in range(ATT_ROWS):
            o_ref[rr * GRID_W + j * COL_BLOCK:rr * GRID_W + (j + 1) * COL_BLOCK, lanes[hh]] = \
                ob[rr * COL_BLOCK:(rr + 1) * COL_BLOCK, :]


def _attention(qkv, bias, tables):
    n = qkv.shape[0]
    gtab, kwin, var = tables
    hw = ATT_HEADS_PER_STEP * HEAD_DIM
    n_hsteps = N_HEADS // ATT_HEADS_PER_STEP
    kv_specs = []
    for part in (1, 2):
        for i in range(ATT_N_KSUB):
            kv_specs.append(pl.BlockSpec(
                (ATT_KSUB_TOK, hw),
                lambda hp, s, gtab, kwin, var, i=i, part=part: (kwin[s] + i, part * n_hsteps + hp)))
    grid_spec = pltpu.PrefetchScalarGridSpec(
        num_scalar_prefetch=3,
        grid=(n_hsteps, len(gtab)),
        in_specs=[pl.BlockSpec((ATT_TOK, hw), lambda hp, s, gtab, kwin, var: (gtab[s], hp))] + kv_specs + [
            pl.BlockSpec((None, ATT_HEADS_PER_STEP, N_COL_BLOCKS, ATT_ROWS * COL_BLOCK, ATT_KROWS * KEY_COLS),
                         lambda hp, s, gtab, kwin, var: (var[s], hp, 0, 0, 0))],
        out_specs=pl.BlockSpec((ATT_TOK, hw), lambda hp, s, gtab, kwin, var: (gtab[s], hp)),
        scratch_shapes=[pltpu.VMEM((ATT_HEADS_PER_STEP, ATT_KROWS * GRID_W, HEAD_DIM), F32),
                        pltpu.VMEM((ATT_HEADS_PER_STEP, ATT_KROWS * GRID_W, HEAD_DIM), F32)],
    )
    return pl.pallas_call(
        _attn_kernel,
        out_shape=jax.ShapeDtypeStruct((n, D_MODEL), BF16),
        grid_spec=grid_spec,
        compiler_params=_cparams(("parallel", "arbitrary")),
        name="nbr_attention",
    )(jnp.asarray(gtab), jnp.asarray(kwin), jnp.asarray(var), qkv, *([qkv] * (2 * ATT_N_KSUB)), bias)


def _first_max4(vals):
    best = vals[0]
    idx = jnp.zeros(best.shape, I32)
    for i in range(1, 4):
        take = vals[i] > best
        best = jnp.where(take, vals[i], best)
        idx = jnp.where(take, I32(i), idx)
    return idx


def _select4(idx, vals):
    return jnp.where(idx == 0, vals[0], jnp.where(idx == 1, vals[1], jnp.where(idx == 2, vals[2], vals[3])))


def _ln_route(z, counted, lng_ref, lnb_ref, rw2_ref, rwhi_ref, rb_ref, tri_ref, x1_ref, ri_ref, rg_ref, cnt_ref,
              carry_ref):
    tm = z.shape[0]

    mu = jnp.mean(z, axis=-1, keepdims=True)
    zc = z - mu
    var = jnp.mean(zc * zc, axis=-1, keepdims=True)
    x1 = zc * lax.rsqrt(var + LN_EPS) * lng_ref[...] + lnb_ref[...]
    x1_ref[...] = x1

    hi = x1.astype(BF16)
    lo = (x1 - hi.astype(F32)).astype(BF16)
    nt = (((1,), (1,)), ((), ()))
    l2 = lax.dot_general(rw2_ref[...], hi, nt, preferred_element_type=F32)
    l1 = lax.dot_general(rwhi_ref[...], lo, nt, preferred_element_type=F32)
    logits = l2[:N_EXPERTS] + l2[N_EXPERTS:] + l1
    ex = jnp.exp(logits - jnp.max(logits, axis=0, keepdims=True))
    scores = ex / jnp.sum(ex, axis=0, keepdims=True)
    biased = scores + rb_ref[...]
    sr = [scores[e:e + 1, :] for e in range(N_EXPERTS)]
    br = [biased[e:e + 1, :] for e in range(N_EXPERTS)]

    group_scores = []
    for g in range(N_EXPERT_GROUPS):
        a, b, c, d = br[g * EXPERTS_PER_GROUP:(g + 1) * EXPERTS_PER_GROUP]
        hi1, lo1, hi2, lo2 = jnp.maximum(a, b), jnp.minimum(a, b), jnp.maximum(c, d), jnp.minimum(c, d)
        group_scores.append(jnp.maximum(hi1, hi2) + jnp.maximum(jnp.minimum(hi1, hi2), jnp.maximum(lo1, lo2)))
    gsel = _first_max4(group_scores)
    vb = [_select4(gsel, [br[g * EXPERTS_PER_GROUP + i] for g in range(N_EXPERT_GROUPS)]) for i in range(4)]
    vs = [_select4(gsel, [sr[g * EXPERTS_PER_GROUP + i] for g in range(N_EXPERT_GROUPS)]) for i in range(4)]
    i1 = _first_max4(vb)
    i2 = _first_max4([jnp.where(i1 == i, -jnp.inf, vb[i]) for i in range(4)])
    e0 = gsel * EXPERTS_PER_GROUP + i1
    e1 = gsel * EXPERTS_PER_GROUP + i2
    w0 = _select4(i1, vs)
    w1 = _select4(i2, vs)
    den = w0 + w1

    eio = lax.broadcasted_iota(I32, (N_EXPERTS, tm), 0)
    oh0 = eio == e0
    oh1 = eio == e1
    ohs = jnp.where(oh0 | oh1, F32(1.0), F32(0.0))
    carry = carry_ref[:, 0:1]
    before = jnp.dot(ohs.astype(BF16), tri_ref[...], preferred_element_type=F32) + carry
    r0 = jnp.sum(jnp.where(oh0, before, 0.0), axis=0, keepdims=True)
    r1 = jnp.sum(jnp.where(oh1, before, 0.0), axis=0, keepdims=True)
    tile_counts = jnp.where(counted, jnp.sum(ohs, axis=1, keepdims=True), 0.0)
    carry_ref[...] = jnp.broadcast_to(carry + tile_counts, carry_ref.shape)
    cnt_ref[...] = carry_ref[...]
    ri_ref[...] = jnp.concatenate([e0, e1, r0.astype(I32), r1.astype(I32), jnp.zeros((4, tm), I32)], axis=0)
    rg_ref[...] = jnp.concatenate([w0 / den, w1 / den, jnp.zeros((6, tm), F32)], axis=0)


def _seq_position(tile, tm, seqs):
    (b0, l0), (_, l1) = seqs
    start = tile * tm
    in_first = start < b0 * l0
    length = jnp.where(in_first, I32(l0), I32(l1))
    pos0 = jnp.where(in_first, start % l0, (start - b0 * l0) % l1)
    return pos0 + lax.broadcasted_iota(I32, (tm, 1), 0), length


def _mixer_tile():
    return jnp.minimum(pl.program_id(0), pl.num_programs(0) - 2)


def _pipelined_post(mix_fn, x_ref, route, y_ref, carry_ref):
    i = pl.program_id(0)

    @pl.when(i == 0)
    def _():
        y_ref[...] = jnp.zeros_like(y_ref)
        carry_ref[...] = jnp.zeros_like(carry_ref)

    z = ALPHA * x_ref[...] + y_ref[...]
    y_ref[...] = mix_fn()
    _ln_route(z, i > 0, *route, carry_ref)


def _post_attn_kernel(o_ref, w_ref, x_ref, *rest):
    *route, carry_ref, y_ref = rest
    _pipelined_post(lambda: jnp.dot(o_ref[...], w_ref[...], preferred_element_type=F32), x_ref, route, y_ref,
                    carry_ref)


def _post_conv_kernel(b_ref, c_ref, h_ref, cp_ref, hp_ref, cn_ref, hn_ref, cw_ref, w_ref, x_ref, *rest, seqs):
    *route, carry_ref, y_ref, u_ref = rest
    _pipelined_post(functools.partial(_conv_mix, b_ref, c_ref, h_ref, cp_ref, hp_ref, cn_ref, hn_ref, cw_ref, w_ref,
                                      u_ref, seqs), x_ref, route, y_ref, carry_ref)


def _conv_mix(b_ref, c_ref, h_ref, cp_ref, hp_ref, cn_ref, hn_ref, cw_ref, w_ref, u_ref, seqs):
    tm = b_ref.shape[0]
    pos, length = _seq_position(_mixer_tile(), tm, seqs)
    h8 = POOL_HALO
    u_before = (cp_ref[...].astype(F32) * hp_ref[...].astype(F32))[CONV_HALO - h8:, :]
    u_after = (cn_ref[...].astype(F32) * hn_ref[...].astype(F32))[0:h8, :]
    u_ref[0:h8, :] = jnp.where(pos[0:1, :] > 0, u_before, 0.0)
    u_ref[h8:h8 + tm, :] = c_ref[...].astype(F32) * h_ref[...].astype(F32)
    u_ref[h8 + tm:, :] = jnp.where(pos[tm - 1:tm, :] + 1 < length, u_after, 0.0)
    prev = u_ref[h8 - 1:h8 - 1 + tm, :]
    nxt = u_ref[h8 + 1:h8 + 1 + tm, :]
    conv = prev * cw_ref[0:1, :] + u_ref[h8:h8 + tm, :] * cw_ref[1:2, :] + nxt * cw_ref[2:3, :]
    zin = (b_ref[...].astype(F32) * conv).astype(BF16)
    return jnp.dot(zin, w_ref[...], preferred_element_type=F32)


def _post_pool_kernel(xm_ref, xp_ref, xn_ref, w_ref, sc_ref, x_ref, *rest, seqs):
    *route, carry_ref, y_ref, xs_ref = rest
    _pipelined_post(functools.partial(_pool_mix, xm_ref, xp_ref, xn_ref, w_ref, sc_ref, xs_ref, seqs), x_ref, route,
                    y_ref, carry_ref)


def _pool_mix(xm_ref, xp_ref, xn_ref, w_ref, sc_ref, xs_ref, seqs):
    tm = xm_ref.shape[0]
    pos, length = _seq_position(_mixer_tile(), tm, seqs)
    h8 = POOL_HALO
    xs_ref[0:h8, :] = jnp.where(pos[0:1, :] > 0, xp_ref[...], 0.0)
    xs_ref[h8:h8 + tm, :] = xm_ref[...]
    xs_ref[h8 + tm:, :] = jnp.where(pos[tm - 1:tm, :] + 1 < length, xn_ref[...], 0.0)
    ys = []
    for g, w in enumerate(POOL_WINDOWS):
        cs = slice(g * POOL_GROUP_DIM, (g + 1) * POOL_GROUP_DIM)
        acc = jnp.zeros((tm, POOL_GROUP_DIM), F32)
        for s in range(-(w // 2), w - w // 2):
            acc = acc + xs_ref[h8 + s:h8 + s + tm, cs]
        lo = jnp.maximum(pos - w // 2, 0)
        hi = jnp.minimum(pos - w // 2 + w, length)
        pooled = acc / (hi - lo).astype(F32) - xm_ref[:, cs]
        ys.append(jnp.dot(pooled.astype(BF16), w_ref[g], preferred_element_type=F32))
    return jnp.concatenate(ys, axis=1) * sc_ref[...]


def _route_operands(router_w, router_b, tm):
    rw_t = router_w.T
    rw_hi = rw_t.astype(BF16)
    rw_lo = (rw_t - rw_hi.astype(F32)).astype(BF16)
    tri = jnp.asarray(np.triu(np.ones((tm, tm), np.float32), 1), BF16)
    return jnp.concatenate([rw_hi, rw_lo], axis=0), rw_hi, router_b.reshape(N_EXPERTS, 1).astype(F32), tri


def _post_call(kind, n, mixer_args, mixer_specs, x, lng, lnb, route_ops, scratch, seqs):
    tm = POST_TM
    rw2, rwhi, rb, tri = route_ops
    const = lambda shape: pl.BlockSpec(shape, lambda i: (0,) * len(shape))
    done = lambda i: jnp.maximum(i - 1, 0)
    in_specs = list(mixer_specs) + [
        pl.BlockSpec((tm, D_MODEL), lambda i: (done(i), 0)),
        const((1, D_MODEL)), const((1, D_MODEL)),
        const((2 * N_EXPERTS, D_MODEL)), const((N_EXPERTS, D_MODEL)), const((N_EXPERTS, 1)), const((tm, tm))]
    out_shape = [jax.ShapeDtypeStruct((n, D_MODEL), F32),
                 jax.ShapeDtypeStruct((8, n), I32),
                 jax.ShapeDtypeStruct((8, n), F32),
                 jax.ShapeDtypeStruct((N_EXPERTS, 128), F32)]
    out_specs = [pl.BlockSpec((tm, D_MODEL), lambda i: (done(i), 0)),
                 pl.BlockSpec((8, tm), lambda i: (0, done(i))),
                 pl.BlockSpec((8, tm), lambda i: (0, done(i))),
                 const((N_EXPERTS, 128))]
    body = {"attn": _post_attn_kernel,
            "conv": functools.partial(_post_conv_kernel, seqs=seqs),
            "pool": functools.partial(_post_pool_kernel, seqs=seqs)}[kind]
    return pl.pallas_call(
        body,
        out_shape=out_shape,
        grid=(n // tm + 1,),
        in_specs=in_specs,
        out_specs=out_specs,
        scratch_shapes=[pltpu.VMEM((N_EXPERTS, 128), F32), pltpu.VMEM((tm, D_MODEL), F32)] + list(scratch),
        compiler_params=_cparams(("arbitrary",)),
        name="post_" + kind,
    )(*mixer_args, x, lng, lnb, rw2, rwhi, rb, tri)


def _pack_bf16_pairs(y):
    half = y.shape[1] // 2
    lo = pltpu.bitcast(y[:, :half].astype(BF16).astype(F32), U32)
    hi = pltpu.bitcast(y[:, half:].astype(BF16).astype(F32), U32)
    return (lo >> 16) | (hi & jnp.uint32(0xFFFF0000))


def _unpack_bf16_pairs(p):
    lo = pltpu.bitcast(p << 16, F32)
    hi = pltpu.bitcast(p & jnp.uint32(0xFFFF0000), F32)
    return jnp.concatenate([lo, hi], axis=1)


def _moe_kernel(bexp, gsrc_ref, sdst_ref, x_hbm, wg_ref, wu_ref, wd_ref, out_hbm, xbuf, ybuf, gsem, ssem):
    del bexp
    groups, sub = ybuf.shape[1], ybuf.shape[2]
    tm = groups * sub
    s = pl.program_id(0)
    last = pl.num_programs(0) - 1
    cur = s % 2
    prev = 1 - cur

    def gather_row(g, u):
        tok = gsrc_ref[0, g * sub + u]
        return pltpu.make_async_copy(x_hbm.at[pl.ds(tok, 1)], xbuf.at[cur, g, pl.ds(u, 1)], gsem.at[cur])

    def scatter_row(g, u):
        row = sdst_ref[0, g * sub + u]
        return pltpu.make_async_copy(ybuf.at[cur, g, pl.ds(u, 1)], out_hbm.at[pl.ds(row, 1)], ssem.at[cur])

    def gather_wait(sl):
        pltpu.make_async_copy(xbuf.at[sl], xbuf.at[sl], gsem.at[sl]).wait()

    def scatter_wait(sl):
        pltpu.make_async_copy(ybuf.at[sl], ybuf.at[sl], ssem.at[sl]).wait()

    @pl.when(s == 0)
    def _():
        ybuf[...] = jnp.zeros_like(ybuf)

    @pl.when(s > 0)
    def _():
        gather_wait(prev)
        scatter_wait(prev)

    def start_rows(g, carry):
        for u in range(sub):
            gather_row(g, u).start()
            scatter_row(g, u).start()
        return carry
    lax.fori_loop(0, groups, start_rows, 0)

    @pl.when((s > 0) & (s < last))
    def _():
        x = xbuf[prev].reshape(tm, D_MODEL).astype(BF16)
        gate = jnp.dot(x, wg_ref[...], preferred_element_type=F32)
        up = jnp.dot(x, wu_ref[...], preferred_element_type=F32)
        hdn = (gate * jax.nn.sigmoid(gate) * up).astype(BF16)
        y = jnp.dot(hdn, wd_ref[...], preferred_element_type=F32)
        ybuf[prev] = _pack_bf16_pairs(y).reshape(groups, sub, D_MODEL // 2)

    @pl.when(s == last)
    def _():
        gather_wait(cur)
        scatter_wait(cur)


def _moe(x1, wg, wu, wd, layer, bexp, gsrc, sdst, n_blocks):
    n = x1.shape[0]
    tm = MOE_TM
    block_of_step = lambda s: jnp.clip(s - 1, 0, n_blocks - 1)
    wspec = lambda shape: pl.BlockSpec((None, None) + shape, lambda s, bexp: (layer, bexp[block_of_step(s)], 0, 0))
    smem_row = lambda lead: pl.BlockSpec((None, 1, tm), lambda s, bexp: (s + lead, 0, 0), memory_space=pltpu.SMEM)
    grid_spec = pltpu.PrefetchScalarGridSpec(
        num_scalar_prefetch=1,
        grid=(n_blocks + 2,),
        in_specs=[smem_row(MOE_LEAD_BLOCKS), smem_row(MOE_LEAD_BLOCKS - 2),
                  pl.BlockSpec(memory_space=pl.ANY),
                  wspec((D_MODEL, D_EXPERT)), wspec((D_MODEL, D_EXPERT)), wspec((D_EXPERT, D_MODEL))],
        out_specs=pl.BlockSpec(memory_space=pl.ANY),
        scratch_shapes=[pltpu.VMEM((2, tm // F32_SUBLANES, F32_SUBLANES, D_MODEL), F32),
                        pltpu.VMEM((2, tm // F32_SUBLANES, F32_SUBLANES, D_MODEL // 2), U32),
                        pltpu.SemaphoreType.DMA((2,)), pltpu.SemaphoreType.DMA((2,))],
    )
    return pl.pallas_call(
        _moe_kernel,
        out_shape=jax.ShapeDtypeStruct((2 * n + tm, D_MODEL // 2), U32),
        grid_spec=grid_spec,
        compiler_params=_cparams(("arbitrary",)),
        name="moe_experts",
    )(bexp, gsrc, sdst, x1, wg, wu, wd)


def _row_table_kernel(dest_ref, fill_hbm, inv_ref, sem):
    fill = pltpu.make_async_copy(fill_hbm, inv_ref, sem)
    fill.start()
    fill.wait()

    def place(i, carry):
        for u in range(TABLE_UNROLL):
            k = i * TABLE_UNROLL + u
            inv_ref[dest_ref[k]] = k
        return carry
    lax.fori_loop(0, dest_ref.shape[0] // TABLE_UNROLL, place, 0)


def _row_table(dest, n_rows):
    assert dest.shape[0] % TABLE_UNROLL == 0
    return pl.pallas_call(
        _row_table_kernel,
        out_shape=jax.ShapeDtypeStruct((n_rows,), I32),
        in_specs=[pl.BlockSpec(memory_space=pltpu.SMEM), pl.BlockSpec(memory_space=pl.ANY)],
        out_specs=pl.BlockSpec(memory_space=pltpu.SMEM),
        scratch_shapes=[pltpu.SemaphoreType.DMA],
        name="row_table",
    )(dest, jnp.full((n_rows,), -1, I32))


def _dispatch_tables(ri, cnt, n):
    tm = MOE_TM
    n_blocks = -(-2 * n // tm) + N_EXPERTS
    n_table_blocks = n_blocks + 2 * MOE_LEAD_BLOCKS
    experts = ri[0:2]
    rank = ri[2:4]
    counts = cnt[:, 0].astype(I32)
    nblk = (counts + tm - 1) // tm
    bend = jnp.cumsum(nblk)
    bstart = bend - nblk
    first_block = jnp.sum(jnp.where(experts[:, :, None] == jnp.arange(N_EXPERTS, dtype=I32), bstart, 0), axis=-1)
    dest = (first_block + MOE_LEAD_BLOCKS) * tm + rank
    rows = jnp.arange(n_table_blocks * tm, dtype=I32)
    inv = _row_table(dest.reshape(-1), rows.shape[0])
    gsrc = jnp.where(inv < 0, n - 1, jnp.where(inv >= n, inv - n, inv))
    sdst = jnp.where(inv < 0, 2 * n + rows % tm, inv)
    blocks = jnp.arange(n_blocks, dtype=I32)
    bexp_raw = jnp.minimum(jnp.sum(blocks[:, None] >= bend[None, :], axis=1), N_EXPERTS - 1).astype(I32)
    bexp = bexp_raw[jnp.minimum(blocks, bend[-1] - 1)]
    shape3 = (n_table_blocks, 1, tm)
    return bexp, gsrc.reshape(shape3), sdst.reshape(shape3), n_blocks


def _ln2_kernel(x_ref, y0_ref, y1_ref, g_ref, lng_ref, lnb_ref, o_ref, *maybe_ob_ref):
    z = (ALPHA * x_ref[...] + _unpack_bf16_pairs(y0_ref[...]) * g_ref[:, 0:1]
         + _unpack_bf16_pairs(y1_ref[...]) * g_ref[:, 1:2])
    mu = jnp.mean(z, axis=-1, keepdims=True)
    zc = z - mu
    var = jnp.mean(zc * zc, axis=-1, keepdims=True)
    out = zc * lax.rsqrt(var + LN_EPS) * lng_ref[...] + lnb_ref[...]
    o_ref[...] = out
    for ob_ref in maybe_ob_ref:
        ob_ref[...] = out.astype(BF16)


def _combine_ln(x1, y2, gates, lng, lnb, *, first_row=0, n_rows=None, with_bf16=True):
    tm = LN2_TM
    n_rows = x1.shape[0] if n_rows is None else n_rows
    assert first_row % tm == 0 and n_rows % tm == 0
    off = first_row // tm
    slot1 = x1.shape[0] // tm
    row = lambda i: (i + off, 0)
    const = lambda i: (0, 0)
    out_shape = [jax.ShapeDtypeStruct((n_rows, D_MODEL), F32)]
    out_specs = [pl.BlockSpec((tm, D_MODEL), lambda i: (i, 0))]
    if with_bf16:
        out_shape.append(jax.ShapeDtypeStruct((n_rows, D_MODEL), BF16))
        out_specs.append(pl.BlockSpec((tm, D_MODEL), lambda i: (i, 0)))
    return pl.pallas_call(
        _ln2_kernel,
        out_shape=out_shape,
        grid=(n_rows // tm,),
        in_specs=[pl.BlockSpec((tm, D_MODEL), row),
                  pl.BlockSpec((tm, D_MODEL // 2), row),
                  pl.BlockSpec((tm, D_MODEL // 2), lambda i: (i + off + slot1, 0)),
                  pl.BlockSpec((tm, 2), row),
                  pl.BlockSpec((1, D_MODEL), const), pl.BlockSpec((1, D_MODEL), const)],
        out_specs=out_specs,
        compiler_params=_cparams(("parallel",)),
        name="combine_ln",
    )(x1, y2, y2, gates, lng, lnb)


def _trunk(x, seqs, attn_w_qkv, attn_w_o, attn_rpb, pool_w, pool_scale, conv_w_in, conv_w, conv_w_out,
           router_w, router_b, expert_w_gate, expert_w_up, expert_w_down, ln_g, ln_b):
    n = x.shape[0]
    tm = POST_TM
    assert n % MM_TM == 0 and n % LN2_TM == 0 and all(l % tm == 0 for _, l in seqs)
    route_ops = _route_operands(router_w, router_b, tm)
    attn_tables = _attn_tables(seqs)
    xb = x.astype(BF16)
    wg_all, wu_all, wd_all = (w.astype(BF16) for w in (expert_w_gate, expert_w_up, expert_w_down))
    mixer_tile = lambda i: jnp.minimum(i, n // tm - 1)
    halo_before = lambda halo: (lambda i: jnp.maximum(mixer_tile(i) * (tm // halo) - 1, 0))
    halo_after = lambda halo: (lambda i: jnp.minimum((mixer_tile(i) + 1) * (tm // halo), n // halo - 1))
    tile_spec = lambda col=0: pl.BlockSpec((tm, D_MODEL), lambda i: (mixer_tile(i), col))
    const2 = lambda i: (0, 0)
    for layer in range(DEPTH):
        kind, slot = layer % N_MIXERS, layer // N_MIXERS
        lng1, lnb1 = ln_g[layer, 0].reshape(1, D_MODEL), ln_b[layer, 0].reshape(1, D_MODEL)
        if kind == 0:
            qkv = _project(xb, attn_w_qkv[slot].astype(BF16), n_scaled=D_MODEL // MM_TN, scale=HEAD_DIM ** -0.5,
                           name="qkv_proj")
            o = _attention(qkv, _attn_bias(attn_rpb[slot]), attn_tables)
            args = (o, attn_w_o[slot].astype(BF16))
            specs = (tile_spec(), pl.BlockSpec((D_MODEL, D_MODEL), const2))
            post = _post_call("attn", n, args, specs, x, lng1, lnb1, route_ops, (), seqs)
        elif kind == 1:
            before, after = halo_before(POOL_HALO), halo_after(POOL_HALO)
            args = (x, x, x, pool_w[slot].astype(BF16), pool_scale[slot].reshape(1, D_MODEL))
            specs = (tile_spec(),
                     pl.BlockSpec((POOL_HALO, D_MODEL), lambda i: (before(i), 0)),
                     pl.BlockSpec((POOL_HALO, D_MODEL), lambda i: (after(i), 0)),
                     pl.BlockSpec((len(POOL_WINDOWS), POOL_GROUP_DIM, POOL_GROUP_DIM), lambda i: (0, 0, 0)),
                     pl.BlockSpec((1, D_MODEL), const2))
            scratch = (pltpu.VMEM((tm + 2 * POOL_HALO, D_MODEL), F32),)
            post = _post_call("pool", n, args, specs, x, lng1, lnb1, route_ops, scratch, seqs)
        else:
            bch = _project(xb, conv_w_in[slot].astype(BF16), name="conv_in_proj")
            before, after = halo_before(CONV_HALO), halo_after(CONV_HALO)
            prev = lambda c: pl.BlockSpec((CONV_HALO, D_MODEL), lambda i, c=c: (before(i), c))
            nxt = lambda c: pl.BlockSpec((CONV_HALO, D_MODEL), lambda i, c=c: (after(i), c))
            args = (bch,) * 7 + (conv_w[slot], conv_w_out[slot].astype(BF16))
            specs = (tile_spec(0), tile_spec(1), tile_spec(2),
                     prev(1), prev(2), nxt(1), nxt(2),
                     pl.BlockSpec((3, D_MODEL), const2),
                     pl.BlockSpec((D_MODEL, D_MODEL), const2))
            scratch = (pltpu.VMEM((tm + 2 * POOL_HALO, D_MODEL), F32),)
            post = _post_call("conv", n, args, specs, x, lng1, lnb1, route_ops, scratch, seqs)
        x1, ri, rg, cnt = post
        bexp, gsrc, sdst, n_blocks = _dispatch_tables(ri, cnt, n)
        y2 = _moe(x1, wg_all, wu_all, wd_all, layer, bexp, gsrc, sdst, n_blocks)
        gates = rg[0:2].T
        lng2, lnb2 = ln_g[layer, 1].reshape(1, D_MODEL), ln_b[layer, 1].reshape(1, D_MODEL)
        if layer + 1 < DEPTH:
            x, xb = _combine_ln(x1, y2, gates, lng2, lnb2)
    n0 = seqs[0][0] * seqs[0][1]
    (y_first,) = _combine_ln(x1, y2, gates, lng2, lnb2, first_row=0, n_rows=n0, with_bf16=False)
    (y_second,) = _combine_ln(x1, y2, gates, lng2, lnb2, first_row=n0, n_rows=n - n0, with_bf16=False)
    return y_first, y_second


def kernel(x_prompt, x_sample, attn_w_qkv, attn_w_o, attn_rpb, pool_w, pool_scale, conv_w_in, conv_w, conv_w_out,
           router_w, router_b, expert_w_gate, expert_w_up, expert_w_down, ln_g, ln_b):
    seqs = (x_prompt.shape[:2], x_sample.shape[:2])
    x = jnp.concatenate([x_prompt.reshape(-1, D_MODEL), x_sample.reshape(-1, D_MODEL)], axis=0)
    y_prompt, y_sample = _trunk(x, seqs, attn_w_qkv, attn_w_o, attn_rpb, pool_w, pool_scale, conv_w_in, conv_w,
                                conv_w_out, router_w, router_b, expert_w_gate, expert_w_up, expert_w_down, ln_g, ln_b)
    return y_prompt.reshape(x_prompt.shape), y_sample.reshape(x_sample.shape)
```
